```python
import math
import jax, jax.numpy as jnp
from jax import lax
import numpy as np

D_MODEL = 1024
BATCH = 2
SEQ = 8192
DEPTH = 4
DEC_BATCH = 128
DEC_SEQ = 1
PAST_LEN = 8192
PAGE_SIZE = 128

N_MIXERS = 3
N_MLA = (DEPTH + 2) // 3
N_RET = (DEPTH + 1) // 3
N_MOBA = DEPTH // 3

DEEPNORM_ALPHA = (2 * DEPTH) ** 0.25
DEEPNORM_BETA = (8 * DEPTH) ** -0.25
LN_EPS = 1e-5
RMS_EPS = 1e-6
Q_BLOCK = 128

MLA_HEADS = 8
MLA_NOPE = 128
MLA_ROPE = 64
MLA_VDIM = 128
MLA_KV_RANK = 256
MLA_Q_RANK = 512
MLA_ROPE_THETA = 10000.0
MLA_SCALE = (MLA_NOPE + MLA_ROPE) ** -0.5

RET_HEADS = 4
RET_DK = 256
RET_DV = 512
RET_CHUNK = 128
RET_ROPE_THETA = 10000.0

MOBA_HEADS = 8
MOBA_HEAD_DIM = 128
MOBA_BLOCK = 256
MOBA_TOPK = 3
ROPE_THETA = 500000.0
ROT_DIM = MOBA_HEAD_DIM // 4
MOBA_SCALE = MOBA_HEAD_DIM ** -0.5

PEER_HEADS = 8
PEER_N_KEYS = 128
PEER_N_EXPERTS = PEER_N_KEYS ** 2
PEER_KEY_DIM = 256
PEER_TOPK = 16
PEER_BLOCK = 128

kernel_name = 'hybrid_mla_retnet_moba_peer_step'

F32 = jnp.float32


def layer_norm(x, g, b):
    xf = x.astype(F32)
    mu = xf.mean(-1, keepdims=True)
    var = jnp.mean(jnp.square(xf - mu), -1, keepdims=True)
    return ((xf - mu) * lax.rsqrt(var + LN_EPS) * g.astype(F32) + b.astype(F32)).astype(x.dtype)


def rms_norm(x, g):
    xf = x.astype(F32)
    return (xf * lax.rsqrt(jnp.mean(xf * xf, -1, keepdims=True) + RMS_EPS) * g.astype(F32)).astype(x.dtype)


def rope(x, pos, theta, rot_dim):
    half = rot_dim // 2
    freqs = jnp.exp(-math.log(theta) * jnp.arange(half, dtype=F32) * (2.0 / rot_dim))
    ang = pos.astype(F32)[:, None] * freqs[None, :]
    cos = jnp.cos(ang)[None, :, None, :]
    sin = jnp.sin(ang)[None, :, None, :]
    xf = x.astype(F32)
    x1 = xf[..., :half]
    x2 = xf[..., half:rot_dim]
    out = jnp.concatenate([x1 * cos - x2 * sin, x2 * cos + x1 * sin, xf[..., rot_dim:]], axis=-1)
    return out.astype(x.dtype)


def masked_softmax(s, mask):
    return jax.nn.softmax(jnp.where(mask, s, -jnp.inf), axis=-1)


def adaln(c, w, b):
    m = jax.nn.silu(c) @ w + b
    shift, scale, gate = jnp.split(m[:, None, :], 3, axis=-1)
    return shift, scale, gate


def modulate(x, shift, scale):
    return x * (1 + scale) + shift


def post_norm(x, y, gate, g, b):
    return layer_norm(DEEPNORM_ALPHA * x + (1 + gate) * y, g, b)


def mla_project(h, pos, w_in, q_norm, kv_norm, w_uq, w_uk):
    B, T, _ = h.shape
    z = h @ w_in
    cq = rms_norm(z[..., :MLA_Q_RANK], q_norm)
    ckv = rms_norm(z[..., MLA_Q_RANK:MLA_Q_RANK + MLA_KV_RANK], kv_norm)
    kpe = rope(z[..., MLA_Q_RANK + MLA_KV_RANK:][:, :, None, :], pos, MLA_ROPE_THETA, MLA_ROPE)[:, :, 0, :]
    q = (cq @ w_uq).reshape(B, T, MLA_HEADS, MLA_NOPE + MLA_ROPE)
    q_lat = jnp.einsum('bthn,rhn->bthr', q[..., :MLA_NOPE], w_uk)
    q_pe = rope(q[..., MLA_NOPE:], pos, MLA_ROPE_THETA, MLA_ROPE)
    return q_lat, q_pe, ckv, kpe


def mla_scores(q_lat, q_pe, ckv, kpe):
    s = jnp.einsum('bthr,bkr->bhtk', q_lat, ckv) + jnp.einsum('bthp,bkp->bhtk', q_pe, kpe)
    return s.astype(F32) * MLA_SCALE


def mla_output(o_lat, w_uv, w_o):
    B, T = o_lat.shape[:2]
    o = jnp.einsum('bthr,rhv->bthv', o_lat, w_uv).reshape(B, T, MLA_HEADS * MLA_VDIM)
    return o @ w_o


def mla_prompt(q_lat, q_pe, ckv, kpe):
    B, S = q_lat.shape[:2]
    nq = S // Q_BLOCK
    kpos = jnp.arange(S)

    def blk(args):
        ql, qp, i = args
        qpos = i * Q_BLOCK + jnp.arange(Q_BLOCK)
        p = masked_softmax(mla_scores(ql, qp, ckv, kpe), kpos[None, :] <= qpos[:, None]).astype(ckv.dtype)
        return jnp.einsum('bhtk,bkr->bthr', p, ckv)

    ql = q_lat.reshape(B, nq, Q_BLOCK, MLA_HEADS, MLA_KV_RANK).swapaxes(0, 1)
    qp = q_pe.reshape(B, nq, Q_BLOCK, MLA_HEADS, MLA_ROPE).swapaxes(0, 1)
    o = lax.map(blk, (ql, qp, jnp.arange(nq)))
    return o.swapaxes(0, 1).reshape(B, S, MLA_HEADS, MLA_KV_RANK)


def mla_sample(q_lat, q_pe, ckv_new, kpe_new, ckv_past, kpe_past):
    T = q_lat.shape[1]
    P = ckv_past.shape[1]
    s = jnp.concatenate([mla_scores(q_lat, q_pe, ckv_past, kpe_past),
                         mla_scores(q_lat, q_pe, ckv_new, kpe_new)], axis=-1)
    mask = jnp.concatenate([jnp.ones((T, P), bool), jnp.tril(jnp.ones((T, T), bool))], axis=-1)
    p = masked_softmax(s, mask).astype(ckv_new.dtype)
    return (jnp.einsum('bhtk,bkr->bthr', p[..., :P], ckv_past)
            + jnp.einsum('bhtk,bkr->bthr', p[..., P:], ckv_new))


def ret_log_gamma():
    return jnp.log1p(-jnp.exp2(-5.0 - jnp.arange(RET_HEADS, dtype=F32)))


def ret_project(h, pos, w_in):
    B, T, _ = h.shape
    hk = RET_HEADS * RET_DK
    hv = RET_HEADS * RET_DV
    z = h @ w_in
    q = rope(z[..., :hk].reshape(B, T, RET_HEADS, RET_DK), pos, RET_ROPE_THETA, RET_DK)
    k = rope(z[..., hk:2 * hk].reshape(B, T, RET_HEADS, RET_DK), pos, RET_ROPE_THETA, RET_DK) * (RET_DK ** -0.5)
    v = z[..., 2 * hk:2 * hk + hv].reshape(B, T, RET_HEADS, RET_DV)
    g = z[..., 2 * hk + hv:]
    return q, k, v, g


def ret_chunk(S, q, k, v):
    L = q.shape[1]
    lg = ret_log_gamma()
    idx = jnp.arange(L, dtype=F32)
    diff = idx[:, None] - idx[None, :]
    decay = jnp.where(diff >= 0, jnp.exp(jnp.maximum(diff, 0.0)[None] * lg[:, None, None]), 0.0)
    inner = jnp.einsum('blhd,bmhd->bhlm', q, k) * decay[None]
    o = jnp.einsum('bhlm,bmhv->blhv', inner, v)
    o = o + jnp.einsum('blhd,bhdv->blhv', q, S) * jnp.exp((idx[:, None] + 1.0) * lg[None, :])[None, :, :, None]
    wk = jnp.exp((L - 1.0 - idx)[:, None] * lg[None, :])
    S_new = jnp.exp(L * lg)[None, :, None, None] * S + jnp.einsum('blhd,blhv->bhdv', k * wk[None, :, :, None], v)
    return S_new, o


def ret_prompt(q, k, v):
    B, S = q.shape[:2]
    nc = S // RET_CHUNK

    def to_chunks(a):
        return a.astype(F32).reshape(B, nc, RET_CHUNK, *a.shape[2:]).swapaxes(0, 1)

    S0 = jnp.zeros((B, RET_HEADS, RET_DK, RET_DV), F32)
    S_fin, o = lax.scan(lambda s, xs: ret_chunk(s, *xs), S0, (to_chunks(q), to_chunks(k), to_chunks(v)))
    return o.swapaxes(0, 1).reshape(B, S, RET_HEADS, RET_DV), S_fin


def ret_output(o, g, gn, w_o):
    B, T = o.shape[:2]
    mu = o.mean(-1, keepdims=True)
    var = jnp.mean(jnp.square(o - mu), -1, keepdims=True)
    y = ((o - mu) * lax.rsqrt(var + LN_EPS)).reshape(B, T, RET_HEADS * RET_DV) * gn.astype(F32)
    return (jax.nn.silu(g.astype(F32)) * y).astype(w_o.dtype) @ w_o


def moba_project(h, pos, w_qkv):
    B, T, _ = h.shape
    z = (h @ w_qkv).reshape(B, T, 3, MOBA_HEADS, MOBA_HEAD_DIM)
    q = rope(z[:, :, 0], pos, ROPE_THETA, ROT_DIM)
    k = rope(z[:, :, 1], pos, ROPE_THETA, ROT_DIM)
    return q, k, z[:, :, 2]


def moba_prompt(q, k, v):
    B, S, H, Dh = q.shape
    nb = -(-S // MOBA_BLOCK)
    pad = nb * MOBA_BLOCK - S
    kp = jnp.pad(k, ((0, 0), (0, pad), (0, 0), (0, 0)))
    vp = jnp.pad(v, ((0, 0), (0, pad), (0, 0), (0, 0)))
    kb = kp.reshape(B, nb, MOBA_BLOCK, H, Dh).transpose(0, 3, 1, 2, 4)
    vb = vp.reshape(B, nb, MOBA_BLOCK, H, Dh).transpose(0, 3, 1, 2, 4)
    kmean = kb.astype(F32).mean(3)
    topk = min(MOBA_TOPK, nb)
    nsel = topk * MOBA_BLOCK
    bidx = jnp.arange(B)[:, None, None, None]
    hidx = jnp.arange(H)[None, :, None, None]
    nq = S // Q_BLOCK

    def blk(args):
        qb, i = args
        start = i * Q_BLOCK
        own = start // MOBA_BLOCK
        qpos = start + jnp.arange(Q_BLOCK)
        gate = jnp.einsum('bqhd,bhnd->bhqn', qb.astype(F32), kmean)
        gate = jnp.where(jnp.arange(nb) < own, gate, -jnp.inf)
        _, sel = lax.top_k(gate, topk)
        sel_ok = sel < own
        kg = kb[bidx, hidx, sel]
        vg = vb[bidx, hidx, sel]
        s_sel = jnp.einsum('bqhd,bhqknd->bhqkn', qb, kg).astype(F32) * MOBA_SCALE
        s_sel = jnp.where(sel_ok[..., None], s_sel, -jnp.inf).reshape(B, H, Q_BLOCK, nsel)
        ko = lax.dynamic_slice_in_dim(kp, own * MOBA_BLOCK, MOBA_BLOCK, axis=1)
        vo = lax.dynamic_slice_in_dim(vp, own * MOBA_BLOCK, MOBA_BLOCK, axis=1)
        kpos = own * MOBA_BLOCK + jnp.arange(MOBA_BLOCK)
        s_own = jnp.einsum('bqhd,bnhd->bhqn', qb, ko).astype(F32) * MOBA_SCALE
        s_own = jnp.where(kpos[None, :] <= qpos[:, None], s_own, -jnp.inf)
        p = jax.nn.softmax(jnp.concatenate([s_sel, s_own], axis=-1), axis=-1).astype(v.dtype)
        p_sel = p[..., :nsel].reshape(B, H, Q_BLOCK, topk, MOBA_BLOCK)
        return (jnp.einsum('bhqkn,bhqknd->bqhd', p_sel, vg)
                + jnp.einsum('bhqn,bnhd->bqhd', p[..., nsel:], vo))

    qs = q.reshape(B, nq, Q_BLOCK, H, Dh).swapaxes(0, 1)
    o = lax.map(blk, (qs, jnp.arange(nq)))
    return o.swapaxes(0, 1).reshape(B, S, H, Dh)


def moba_sample(q, k, v, pool_k, pool_v, layer, page_table):
    DB, T, H, Dh = q.shape
    past = page_table.shape[1] * PAGE_SIZE
    ppb = MOBA_BLOCK // PAGE_SIZE
    own = past // MOBA_BLOCK
    own_past = past - own * MOBA_BLOCK
    scores, values, specs = [], [], []
    if own > 0:
        kmean = lax.map(
            lambda pt: pool_k[layer, pt[:own * ppb]].astype(F32).reshape(own, ppb, H, PAGE_SIZE, Dh).mean(axis=(1, 3)),
            page_table)
        topk = min(MOBA_TOPK, own)
        gate = jnp.einsum('bthd,bnhd->bhtn', q.astype(F32), kmean)
        _, sel = lax.top_k(gate, topk)
        logical = sel[..., None] * ppb + jnp.arange(ppb)
        phys = page_table[jnp.arange(DB)[:, None, None, None, None], logical]
        hidx = jnp.arange(H)[None, :, None, None, None]
        kg = pool_k[layer, phys, hidx].reshape(DB, H, T, topk * MOBA_BLOCK, Dh)
        vg = pool_v[layer, phys, hidx].reshape(DB, H, T, topk * MOBA_BLOCK, Dh)
        scores.append(jnp.einsum('bthd,bhtnd->bhtn', q, kg))
        values.append(vg)
        specs.append('bhtn,bhtnd->bthd')
    if own_past > 0:
        pages = page_table[:, own * ppb:]
        ko = pool_k[layer, pages].swapaxes(1, 2).reshape(DB, H, own_past, Dh)
        vo = pool_v[layer, pages].swapaxes(1, 2).reshape(DB, H, own_past, Dh)
        scores.append(jnp.einsum('bthd,bhnd->bhtn', q, ko))
        values.append(vo)
        specs.append('bhtn,bhnd->bthd')
    scores.append(jnp.einsum('bthd,bnhd->bhtn', q, k))
    values.append(v)
    specs.append('bhtn,bnhd->bthd')
    s = jnp.concatenate([sc.astype(F32) for sc in scores], axis=-1) * MOBA_SCALE
    n_prev = s.shape[-1] - T
    mask = jnp.concatenate([jnp.ones((T, n_prev), bool), jnp.tril(jnp.ones((T, T), bool))], axis=-1)
    p = masked_softmax(s, mask).astype(v.dtype)
    outs = []
    off = 0
    for sc, val, spec in zip(scores, values, specs):
        n = sc.shape[-1]
        outs.append(jnp.einsum(spec, p[..., off:off + n], val))
        off += n
    out = outs[0]
    for o in outs[1:]:
        out = out + o
    return out


def peer(h, w_q, k1, k2, u_tab, v_tab):
    N = h.shape[0]
    half = PEER_KEY_DIM // 2
    q = (h @ w_q).reshape(N, PEER_HEADS, PEER_KEY_DIM).astype(F32)
    s1 = jnp.einsum('nhd,kd->nhk', q[..., :half], k1.astype(F32))
    s2 = jnp.einsum('nhd,kd->nhk', q[..., half:], k2.astype(F32))
    t1, i1 = lax.top_k(s1, PEER_TOPK)
    t2, i2 = lax.top_k(s2, PEER_TOPK)
    cand_s = (t1[..., :, None] + t2[..., None, :]).reshape(N, PEER_HEADS, PEER_TOPK * PEER_TOPK)
    cand_i = (i1[..., :, None] * PEER_N_KEYS + i2[..., None, :]).reshape(N, PEER_HEADS, PEER_TOPK * PEER_TOPK)
    top_s, pos = lax.top_k(cand_s, PEER_TOPK)
    eidx = jnp.take_along_axis(cand_i, pos, axis=-1)
    g = jax.nn.softmax(top_s, axis=-1)
    u = u_tab[eidx]
    a = jax.nn.gelu(jnp.einsum('nd,nhkd->nhk', h, u).astype(F32), approximate=False)
    vv = v_tab[eidx]
    return jnp.einsum('nhk,nhkd->nd', (g * a).astype(h.dtype), vv)


def peer_prompt(h, w_q, k1, k2, u_tab, v_tab):
    B, S, D = h.shape
    out = lax.map(lambda hb: peer(hb, w_q, k1, k2, u_tab, v_tab), h.reshape(-1, PEER_BLOCK, D))
    return out.reshape(B, S, D)


def setup_inputs(seed: int = 0) -> dict:
    key = jax.random.key(seed)
    ks = iter(jax.random.split(key, 40))
    n_pages = PAST_LEN // PAGE_SIZE
    n_pool = (DEC_BATCH * n_pages * 5) // 4
    D = D_MODEL
    beta = DEEPNORM_BETA

    def nrm(shape, scale):
        return jax.random.normal(next(ks), shape, F32) * scale

    perm = jax.random.permutation(next(ks), n_pool)
    page_table = perm[:DEC_BATCH * n_pages].reshape(DEC_BATCH, n_pages).astype(jnp.int32)
    return {
        'x_prompt': nrm((BATCH, SEQ, D), 1.0),
        'x_sample': nrm((DEC_BATCH, DEC_SEQ, D), 1.0),
        'cache_mla_ckv': nrm((N_MLA, n_pool, PAGE_SIZE, MLA_KV_RANK), 1.0),
        'cache_mla_kpe': nrm((N_MLA, n_pool, PAGE_SIZE, MLA_ROPE), 1.0),
        'cache_moba_k': nrm((N_MOBA, n_pool, MOBA_HEADS, PAGE_SIZE, MOBA_HEAD_DIM), 1.0),
        'cache_moba_v': nrm((N_MOBA, n_pool, MOBA_HEADS, PAGE_SIZE, MOBA_HEAD_DIM), 1.0),
        'state_ret': nrm((N_RET, DEC_BATCH, RET_HEADS, RET_DK, RET_DV), 0.1),
        'page_table': page_table,
        'c_prompt': nrm((BATCH, D), 1.0),
        'c_sample': nrm((DEC_BATCH, D), 1.0),
        'ada_w': nrm((DEPTH, 2, D, 3 * D), 0.1 * D ** -0.5),
        'ada_b': nrm((DEPTH, 2, 3 * D), 0.01),
        'ln_g': 1.0 + nrm((DEPTH, 2, D), 0.01),
        'ln_b': nrm((DEPTH, 2, D), 0.01),
        'mla_w_in': nrm((N_MLA, D, MLA_Q_RANK + MLA_KV_RANK + MLA_ROPE), D ** -0.5),
        'mla_q_norm': 1.0 + nrm((N_MLA, MLA_Q_RANK), 0.01),
        'mla_kv_norm': 1.0 + nrm((N_MLA, MLA_KV_RANK), 0.01),
        'mla_w_uq': nrm((N_MLA, MLA_Q_RANK, MLA_HEADS * (MLA_NOPE + MLA_ROPE)), MLA_Q_RANK ** -0.5),
        'mla_w_uk': nrm((N_MLA, MLA_KV_RANK, MLA_HEADS, MLA_NOPE), MLA_KV_RANK ** -0.5),
        'mla_w_uv': nrm((N_MLA, MLA_KV_RANK, MLA_HEADS, MLA_VDIM), MLA_KV_RANK ** -0.5),
        'mla_w_o': nrm((N_MLA, MLA_HEADS * MLA_VDIM, D), beta * (MLA_HEADS * MLA_VDIM) ** -0.5),
        'ret_w_in': nrm((N_RET, D, 2 * RET_HEADS * RET_DK + 2 * RET_HEADS * RET_DV), D ** -0.5),
        'ret_gn': 1.0 + nrm((N_RET, RET_HEADS * RET_DV), 0.01),
        'ret_w_o': nrm((N_RET, RET_HEADS * RET_DV, D), beta * (RET_HEADS * RET_DV) ** -0.5),
        'moba_w_qkv': nrm((N_MOBA, D, 3 * MOBA_HEADS * MOBA_HEAD_DIM), D ** -0.5),
        'moba_w_o': nrm((N_MOBA, MOBA_HEADS * MOBA_HEAD_DIM, D), beta * (MOBA_HEADS * MOBA_HEAD_DIM) ** -0.5),
        'peer_w_q': nrm((DEPTH, D, PEER_HEADS * PEER_KEY_DIM), D ** -0.5),
        'peer_k1': nrm((DEPTH, PEER_N_KEYS, PEER_KEY_DIM // 2), (PEER_KEY_DIM // 2) ** -0.5),
        'peer_k2': nrm((DEPTH, PEER_N_KEYS, PEER_KEY_DIM // 2), (PEER_KEY_DIM // 2) ** -0.5),
        'peer_u': nrm((DEPTH, PEER_N_EXPERTS, D), D ** -0.5),
        'peer_v': nrm((DEPTH, PEER_N_EXPERTS, D), beta * PEER_HEADS ** -0.5),
    }


def reference(x_prompt, x_sample, cache_mla_ckv, cache_mla_kpe, cache_moba_k, cache_moba_v, state_ret,
              page_table, c_prompt, c_sample, ada_w, ada_b, ln_g, ln_b,
              mla_w_in, mla_q_norm, mla_kv_norm, mla_w_uq, mla_w_uk, mla_w_uv, mla_w_o,
              ret_w_in, ret_gn, ret_w_o, moba_w_qkv, moba_w_o,
              peer_w_q, peer_k1, peer_k2, peer_u, peer_v):
    B, S, D = x_prompt.shape
    DB, T, _ = x_sample.shape
    past = page_table.shape[1] * PAGE_SIZE
    pos_p = jnp.arange(S)
    pos_s = past + jnp.arange(T)
    xp, xs = x_prompt, x_sample
    mla_ckv_p, mla_kpe_p, mla_ckv_s, mla_kpe_s = [], [], [], []
    moba_k_p, moba_v_p, moba_k_s, moba_v_s = [], [], [], []
    ret_s_p, ret_s_s = [], []

    for i in range(DEPTH):
        j = i // N_MIXERS
        kind = i % N_MIXERS
        shp, scp, gtp = adaln(c_prompt, ada_w[i, 0], ada_b[i, 0])
        shs, scs, gts = adaln(c_sample, ada_w[i, 0], ada_b[i, 0])
        hp = modulate(xp, shp, scp)
        hs = modulate(xs, shs, scs)
        if kind == 0:
            qlp, qpp, ckvp, kpep = mla_project(hp, pos_p, mla_w_in[j], mla_q_norm[j], mla_kv_norm[j], mla_w_uq[j], mla_w_uk[j])
            mp = mla_output(mla_prompt(qlp, qpp, ckvp, kpep), mla_w_uv[j], mla_w_o[j])
            qls, qps, ckvs, kpes = mla_project(hs, pos_s, mla_w_in[j], mla_q_norm[j], mla_kv_norm[j], mla_w_uq[j], mla_w_uk[j])
            ckv_past = cache_mla_ckv[j, page_table].reshape(DB, past, MLA_KV_RANK)
            kpe_past = cache_mla_kpe[j, page_table].reshape(DB, past, MLA_ROPE)
            ms = mla_output(mla_sample(qls, qps, ckvs, kpes, ckv_past, kpe_past), mla_w_uv[j], mla_w_o[j])
            mla_ckv_p.append(ckvp)
            mla_kpe_p.append(kpep)
            mla_ckv_s.append(ckvs)
            mla_kpe_s.append(kpes)
        elif kind == 1:
            qp_, kp_, vp_, gp_ = ret_project(hp, pos_p, ret_w_in[j])
            op_, Sp = ret_prompt(qp_, kp_, vp_)
            mp = ret_output(op_, gp_, ret_gn[j], ret_w_o[j])
            qs_, ks_, vs_, gs_ = ret_project(hs, pos_s, ret_w_in[j])
            Ss, os_ = ret_chunk(state_ret[j].astype(F32), qs_.astype(F32), ks_.astype(F32), vs_.astype(F32))
            ms = ret_output(os_, gs_, ret_gn[j], ret_w_o[j])
            ret_s_p.append(Sp.astype(state_ret.dtype))
            ret_s_s.append(Ss.astype(state_ret.dtype))
        else:
            qp_, kp_, vp_ = moba_project(hp, pos_p, moba_w_qkv[j])
            mp = moba_prompt(qp_, kp_, vp_).reshape(B, S, MOBA_HEADS * MOBA_HEAD_DIM) @ moba_w_o[j]
            qs_, ks_, vs_ = moba_project(hs, pos_s, moba_w_qkv[j])
            ms = moba_sample(qs_, ks_, vs_, cache_moba_k, cache_moba_v, j, page_table).reshape(DB, T, MOBA_HEADS * MOBA_HEAD_DIM) @ moba_w_o[j]
            moba_k_p.append(kp_.swapaxes(1, 2))
            moba_v_p.append(vp_.swapaxes(1, 2))
            moba_k_s.append(ks_.swapaxes(1, 2))
            moba_v_s.append(vs_.swapaxes(1, 2))
        xp = post_norm(xp, mp, gtp, ln_g[i, 0], ln_b[i, 0])
        xs = post_norm(xs, ms, gts, ln_g[i, 0], ln_b[i, 0])
        shp, scp, gtp = adaln(c_prompt, ada_w[i, 1], ada_b[i, 1])
        shs, scs, gts = adaln(c_sample, ada_w[i, 1], ada_b[i, 1])
        hp = modulate(xp, shp, scp)
        hs = modulate(xs, shs, scs)
        fp = peer_prompt(hp, peer_w_q[i], peer_k1[i], peer_k2[i], peer_u[i], peer_v[i])
        fs = peer(hs.reshape(DB * T, D), peer_w_q[i], peer_k1[i], peer_k2[i], peer_u[i], peer_v[i]).reshape(DB, T, D)
        xp = post_norm(xp, fp, gtp, ln_g[i, 1], ln_b[i, 1])
        xs = post_norm(xs, fs, gts, ln_g[i, 1], ln_b[i, 1])

    return (xp, xs,
            jnp.stack(mla_ckv_p), jnp.stack(mla_kpe_p), jnp.stack(mla_ckv_s), jnp.stack(mla_kpe_s),
            jnp.stack(moba_k_p), jnp.stack(moba_v_p), jnp.stack(moba_k_s), jnp.stack(moba_v_s),
            jnp.stack(ret_s_p), jnp.stack(ret_s_s))
```

```python
import functools
import math

import jax
import jax.numpy as jnp
from jax import lax
from jax.experimental import pallas as pl
from jax.experimental.pallas import tpu as pltpu

F32 = jnp.float32
BF16 = jnp.bfloat16

N_MIXERS = 3
LN_EPS = 1e-5
RMS_EPS = 1e-6
PAGE_SIZE = 128

MLA_HEADS = 8
MLA_NOPE = 128
MLA_ROPE = 64
MLA_VDIM = 128
MLA_KV_RANK = 256
MLA_Q_RANK = 512
MLA_ROPE_THETA = 10000.0
MLA_SCALE = (MLA_NOPE + MLA_ROPE) ** -0.5

RET_HEADS = 4
RET_DK = 256
RET_DV = 512
RET_CHUNK = 128
RET_ROPE_THETA = 10000.0

MOBA_HEADS = 8
MOBA_HEAD_DIM = 128
MOBA_BLOCK = 256
MOBA_TOPK = 3
MOBA_ROPE_THETA = 500000.0
MOBA_ROT_DIM = MOBA_HEAD_DIM // 4
MOBA_SCALE = MOBA_HEAD_DIM ** -0.5

PEER_HEADS = 8
PEER_N_KEYS = 128
PEER_KEY_DIM = 256
PEER_TOPK = 16

LANES = 128
V7X_VMEM_BYTES = 64 * 1024 * 1024
BIG_VMEM_LIMIT = V7X_VMEM_BYTES * 3 // 4
NEG = -1e30
INV_SQRT2 = 0.7071067811865476

_NT = (((1,), (1,)), ((), ()))


def _cp(sem, vmem=None):
    kw = dict(dimension_semantics=sem)
    if vmem is not None:
        kw["vmem_limit_bytes"] = vmem
    return pltpu.CompilerParams(**kw)


def _tok_spec(tm, f):
    return pl.BlockSpec((tm, f), lambda i, *_: (i, 0))


def _mod_spec(mod, n, tm):
    g, r, d = mod.shape
    tiles_per_group = n // (g * tm)
    return pl.BlockSpec((1, r, d), lambda i, *_: (i // tiles_per_group, 0, 0))


def _tab_spec(tab, tm):
    p, w = tab.shape
    if p == 1:
        return pl.BlockSpec((1, w), lambda i, *_: (0, 0))
    nt = p // tm
    return pl.BlockSpec((tm, w), lambda i, *_: (i % nt, 0))


def _full_spec(a):
    nd = a.ndim
    return pl.BlockSpec(a.shape, lambda *_: (0,) * nd)


def _rope_tables(pos, theta, rot_dim, period, reps):
    half = rot_dim // 2
    freqs = jnp.exp(-math.log(theta) * jnp.arange(half, dtype=F32) * (2.0 / rot_dim))
    ang = pos.astype(F32)[:, None] * freqs[None, :]
    cos, sin = jnp.cos(ang), jnp.sin(ang)
    p = pos.shape[0]
    rest = period - rot_dim
    c = jnp.concatenate([cos, cos, jnp.ones((p, rest), F32)], axis=1)
    sa = jnp.concatenate([-sin, jnp.zeros((p, period - half), F32)], axis=1)
    sb = jnp.concatenate([jnp.zeros((p, half), F32), sin, jnp.zeros((p, rest), F32)], axis=1)
    return tuple(jnp.tile(t, (1, reps)) for t in (c, sa, sb))


def _rope(x, c, sa, sb, half):
    w = x.shape[-1]
    return x * c + pltpu.roll(x, w - half, 1) * sa + pltpu.roll(x, half, 1) * sb


def _post_norm(x, f, gate, g, b, alpha):
    y = alpha * x + (1.0 + gate) * f
    mu = jnp.mean(y, axis=-1, keepdims=True)
    yc = y - mu
    var = jnp.mean(yc * yc, axis=-1, keepdims=True)
    return yc * lax.rsqrt(var + LN_EPS) * g + b


def _rms(x):
    return x * lax.rsqrt(jnp.mean(x * x, axis=-1, keepdims=True) + RMS_EPS)


def _adaln_kernel(c_ref, w_ref, b_ref, o_ref):
    c = c_ref[...]
    a = (c * jax.nn.sigmoid(c)).astype(BF16)
    o_ref[0] = jnp.dot(a, w_ref[0].astype(BF16), preferred_element_type=F32) + b_ref[0]


def _adaln(c_all, ada_w, ada_b):
    nl = ada_w.shape[0] * ada_w.shape[1]
    d, f = ada_w.shape[2], ada_w.shape[3]
    w = ada_w.reshape(nl, d, f)
    b = ada_b.reshape(nl, 1, f)
    r = c_all.shape[0]
    tn = 1024
    return pl.pallas_call(
        _adaln_kernel,
        grid=(nl, f // tn),
        in_specs=[pl.BlockSpec((r, d), lambda l, j: (0, 0)),
                  pl.BlockSpec((1, d, tn), lambda l, j: (l, 0, j)),
                  pl.BlockSpec((1, 1, tn), lambda l, j: (l, 0, j))],
        out_specs=pl.BlockSpec((1, r, tn), lambda l, j: (l, 0, j)),
        out_shape=jax.ShapeDtypeStruct((nl, r, f), F32),
        compiler_params=_cp(("parallel", "parallel")),
        name="adaln",
    )(c_all, w, b)


def _mla_proj_kernel(x_ref, sh_ref, sc_ref, win_ref, qn_ref, kvn_ref, wuq_ref, wuk_ref,
                     cq_ref, sqa_ref, sqb_ref, ck_ref, ska_ref, skb_ref,
                     ckv_ref, kpe_ref, ckvb_ref, kpeb_ref, ql_ref, qp_ref):
    h = (x_ref[...] * (1.0 + sc_ref[0]) + sh_ref[0]).astype(BF16)
    z = jnp.dot(h, win_ref[...], preferred_element_type=F32)
    cq = _rms(z[:, :MLA_Q_RANK]) * qn_ref[...]
    ckv = _rms(z[:, MLA_Q_RANK:MLA_Q_RANK + MLA_KV_RANK]) * kvn_ref[...]
    kx = z[:, MLA_Q_RANK + MLA_KV_RANK:]
    kpe = _rope(kx, ck_ref[...], ska_ref[...], skb_ref[...], MLA_ROPE // 2)[:, :MLA_ROPE]
    ckv_ref[...] = ckv
    kpe_ref[...] = kpe
    ckvb_ref[...] = ckv.astype(BF16)
    kpeb_ref[...] = kpe.astype(BF16)
    q = jnp.dot(cq.astype(BF16), wuq_ref[...], preferred_element_type=F32)
    for hh in range(MLA_HEADS):
        qn = q[:, hh * MLA_NOPE:(hh + 1) * MLA_NOPE].astype(BF16)
        ql = jnp.dot(qn, wuk_ref[hh], preferred_element_type=F32) * MLA_SCALE
        ql_ref[:, hh * MLA_KV_RANK:(hh + 1) * MLA_KV_RANK] = ql.astype(BF16)
    qx = q[:, MLA_HEADS * MLA_NOPE:]
    qp = _rope(qx, cq_ref[...], sqa_ref[...], sqb_ref[...], MLA_ROPE // 2) * MLA_SCALE
    qp_ref[...] = qp.astype(BF16)


def _mla_proj(x, sh, sc, w, tabs_q, tabs_k, tm):
    n, d = x.shape
    hd = MLA_HEADS
    ins = [x, sh, sc, w["w_in"], w["q_norm"], w["kv_norm"], w["w_uq"], w["w_uk"], *tabs_q, *tabs_k]
    in_specs = ([_tok_spec(tm, d), _mod_spec(sh, n, tm), _mod_spec(sc, n, tm)]
                + [_full_spec(a) for a in ins[3:8]]
                + [_tab_spec(t, tm) for t in ins[8:]])
    outs = [(MLA_KV_RANK, F32), (MLA_ROPE, F32), (MLA_KV_RANK, BF16), (MLA_ROPE, BF16),
            (hd * MLA_KV_RANK, BF16), (hd * MLA_ROPE, BF16)]
    return pl.pallas_call(
        _mla_proj_kernel,
        grid=(n // tm,),
        in_specs=in_specs,
        out_specs=[_tok_spec(tm, f) for f, _ in outs],
        out_shape=[jax.ShapeDtypeStruct((n, f), dt) for f, dt in outs],
        compiler_params=_cp(("parallel",), BIG_VMEM_LIMIT),
        name="mla_proj",
    )(*ins)


def _mla_attn_kernel(ql_ref, qp_ref, k_ref, kp_ref, o_ref, m_sc, l_sc, acc_sc, *, tq, tk, nk):
    i = pl.program_id(1)
    j = pl.program_id(2)
    rows = tq * MLA_HEADS

    @pl.when(j == 0)
    def _():
        m_sc[...] = jnp.full((rows, 1), NEG, F32)
        l_sc[...] = jnp.zeros((rows, 1), F32)
        acc_sc[...] = jnp.zeros((rows, MLA_KV_RANK), F32)

    @pl.when(j * tk <= i * tq + tq - 1)
    def _():
        k = k_ref[0]
        s = (lax.dot_general(ql_ref[0], k, _NT, preferred_element_type=F32)
             + lax.dot_general(qp_ref[0], kp_ref[0], _NT, preferred_element_type=F32))
        qpos = i * tq + (lax.broadcasted_iota(jnp.int32, (rows, tk), 0) // MLA_HEADS)
        kpos = j * tk + lax.broadcasted_iota(jnp.int32, (rows, tk), 1)
        s = jnp.where(kpos <= qpos, s, NEG)
        m_prev = m_sc[...]
        m_new = jnp.maximum(m_prev, jnp.max(s, axis=1, keepdims=True))
        alpha = jnp.exp(m_prev - m_new)
        p = jnp.exp(s - m_new)
        l_sc[...] = alpha * l_sc[...] + jnp.sum(p, axis=1, keepdims=True)
        acc_sc[...] = alpha * acc_sc[...] + jnp.dot(p.astype(BF16), k, preferred_element_type=F32)
        m_sc[...] = m_new

    @pl.when(j == nk - 1)
    def _():
        o_ref[0] = (acc_sc[...] / l_sc[...]).astype(o_ref.dtype)


def _mla_attn(ql, qp, kb, kpb, b, s):
    hd = MLA_HEADS
    tq = min(128, s)
    tk = min(512, s)
    nq, nk = s // tq, s // tk
    rows = tq * hd

    def kmap(bb, i, j):
        return (bb, jnp.minimum(j, (i * tq + tq - 1) // tk), 0)

    return pl.pallas_call(
        functools.partial(_mla_attn_kernel, tq=tq, tk=tk, nk=nk),
        grid=(b, nq, nk),
        in_specs=[pl.BlockSpec((1, rows, MLA_KV_RANK), lambda bb, i, j: (bb, i, 0)),
                  pl.BlockSpec((1, rows, MLA_ROPE), lambda bb, i, j: (bb, i, 0)),
                  pl.BlockSpec((1, tk, MLA_KV_RANK), kmap),
                  pl.BlockSpec((1, tk, MLA_ROPE), kmap)],
        out_specs=pl.BlockSpec((1, rows, MLA_KV_RANK), lambda bb, i, j: (bb, i, 0)),
        out_shape=jax.ShapeDtypeStruct((b, s * hd, MLA_KV_RANK), BF16),
        scratch_shapes=[pltpu.VMEM((rows, 1), F32), pltpu.VMEM((rows, 1), F32),
                        pltpu.VMEM((rows, MLA_KV_RANK), F32)],
        compiler_params=_cp(("parallel", "parallel", "arbitrary"), BIG_VMEM_LIMIT),
        name="mla_attn",
    )(ql, qp, kb, kpb)


MLA_DEC_ROWS = 16
MLA_DEC_PAGES = 8


def _mla_dec_kernel(pt_ref, ql_ref, qp_ref, kn_ref, kpn_ref, *rest):
    g_pages = MLA_DEC_PAGES
    pages = rest[:g_pages]
    ppages = rest[g_pages:2 * g_pages]
    o_ref = rest[2 * g_pages]
    m_sc, l_sc, acc_sc = rest[2 * g_pages + 1:]
    g = pl.program_id(1)
    ql = ql_ref[0]
    qp = qp_ref[0]

    @pl.when(g == 0)
    def _():
        kn = kn_ref[0]
        s_self = (jnp.sum(ql.astype(F32) * kn, axis=1, keepdims=True)
                  + jnp.sum(qp.astype(F32) * kpn_ref[0], axis=1, keepdims=True))
        m_sc[...] = s_self
        l_sc[...] = jnp.ones_like(s_self)
        acc_sc[...] = jnp.broadcast_to(kn, acc_sc.shape)

    kbs = []
    ss = []
    for t in range(g_pages):
        kb = pages[t][0, 0].astype(BF16)
        kpb = ppages[t][0, 0].astype(BF16)
        kbs.append(kb)
        ss.append(lax.dot_general(ql, kb, _NT, preferred_element_type=F32)
                  + lax.dot_general(qp, kpb, _NT, preferred_element_type=F32))
    s = jnp.concatenate(ss, axis=1)
    m_prev = m_sc[...]
    m_new = jnp.maximum(m_prev, jnp.max(s, axis=1, keepdims=True))
    alpha = jnp.exp(m_prev - m_new)
    p = jnp.exp(s - m_new)
    l_sc[...] = alpha * l_sc[...] + jnp.sum(p, axis=1, keepdims=True)
    acc = alpha * acc_sc[...]
    for t in range(g_pages):
        acc = acc + jnp.dot(p[:, t * PAGE_SIZE:(t + 1) * PAGE_SIZE].astype(BF16), kbs[t],
                            preferred_element_type=F32)
    acc_sc[...] = acc
    m_sc[...] = m_new

    @pl.when(g == pl.num_programs(1) - 1)
    def _():
        o_ref[0] = (acc_sc[...] / l_sc[...]).astype(o_ref.dtype)


def _mla_dec(page_table, ql, qp, kn, kpn, cache_ckv, cache_kpe, layer):
    db, n_pages = page_table.shape
    gp = MLA_DEC_PAGES
    assert n_pages % gp == 0
    rows = MLA_DEC_ROWS

    def page_spec(width, t):
        return pl.BlockSpec((1, 1, PAGE_SIZE, width),
                            lambda bb, g, pt: (layer, pt[bb, g * gp + t], 0, 0))

    in_specs = ([pl.BlockSpec((1, rows, MLA_KV_RANK), lambda bb, g, pt: (bb, 0, 0)),
                 pl.BlockSpec((1, rows, MLA_ROPE), lambda bb, g, pt: (bb, 0, 0)),
                 pl.BlockSpec((1, 1, MLA_KV_RANK), lambda bb, g, pt: (bb, 0, 0)),
                 pl.BlockSpec((1, 1, MLA_ROPE), lambda bb, g, pt: (bb, 0, 0))]
                + [page_spec(MLA_KV_RANK, t) for t in range(gp)]
                + [page_spec(MLA_ROPE, t) for t in range(gp)])
    return pl.pallas_call(
        _mla_dec_kernel,
        grid_spec=pltpu.PrefetchScalarGridSpec(
            num_scalar_prefetch=1,
            grid=(db, n_pages // gp),
            in_specs=in_specs,
            out_specs=pl.BlockSpec((1, rows, MLA_KV_RANK), lambda bb, g, pt: (bb, 0, 0)),
            scratch_shapes=[pltpu.VMEM((rows, 1), F32), pltpu.VMEM((rows, 1), F32),
                            pltpu.VMEM((rows, MLA_KV_RANK), F32)]),
        out_shape=jax.ShapeDtypeStruct((db, rows, MLA_KV_RANK), BF16),
        compiler_params=_cp(("parallel", "arbitrary")),
        name="mla_dec",
    )(page_table, ql, qp, kn, kpn, *([cache_ckv] * gp), *([cache_kpe] * gp))


def _mla_out_kernel(o_ref, wuv_ref, wo_ref, x_ref, gt_ref, g_ref, b_ref, y_ref, t_sc, *, alpha):
    for hh in range(MLA_HEADS):
        t = jnp.dot(o_ref[:, hh * MLA_KV_RANK:(hh + 1) * MLA_KV_RANK], wuv_ref[hh],
                    preferred_element_type=F32)
        t_sc[:, hh * MLA_VDIM:(hh + 1) * MLA_VDIM] = t.astype(BF16)
    f = jnp.dot(t_sc[...], wo_ref[...], preferred_element_type=F32)
    y_ref[...] = _post_norm(x_ref[...], f, gt_ref[0], g_ref[...], b_ref[...], alpha)


def _mla_out(o, w, x, gate, ln_g, ln_b, alpha, tm):
    n, d = x.shape
    ins = [o, w["w_uv"], w["w_o"], x, gate, ln_g, ln_b]
    return pl.pallas_call(
        functools.partial(_mla_out_kernel, alpha=alpha),
        grid=(n // tm,),
        in_specs=[_tok_spec(tm, o.shape[1]), _full_spec(ins[1]), _full_spec(ins[2]),
                  _tok_spec(tm, d), _mod_spec(gate, n, tm), _full_spec(ln_g), _full_spec(ln_b)],
        out_specs=_tok_spec(tm, d),
        out_shape=jax.ShapeDtypeStruct((n, d), F32),
        scratch_shapes=[pltpu.VMEM((tm, MLA_HEADS * MLA_VDIM), BF16)],
        compiler_params=_cp(("parallel",), BIG_VMEM_LIMIT),
        name="mla_out",
    )(*ins)


RET_COLS = 1024


def _ret_proj_kernel(x_ref, sh_ref, sc_ref, w_ref, c_ref, sa_ref, sb_ref, z_ref):
    j = pl.program_id(1)
    h = (x_ref[...] * (1.0 + sc_ref[0]) + sh_ref[0]).astype(BF16)
    z = jnp.dot(h, w_ref[...], preferred_element_type=F32)

    @pl.when(j < 2)
    def _():
        kscale = jnp.where(j == 1, RET_DK ** -0.5, 1.0).astype(F32)
        z_ref[...] = _rope(z, c_ref[...], sa_ref[...], sb_ref[...], RET_DK // 2) * kscale

    @pl.when(j >= 2)
    def _():
        z_ref[...] = z


def _ret_proj(x, sh, sc, w_in, tabs, tm):
    n, d = x.shape
    f = w_in.shape[1]
    assert RET_HEADS * RET_DK == RET_COLS
    return pl.pallas_call(
        _ret_proj_kernel,
        grid=(n // tm, f // RET_COLS),
        in_specs=[_tok_spec(tm, d), _mod_spec(sh, n, tm), _mod_spec(sc, n, tm),
                  pl.BlockSpec((d, RET_COLS), lambda i, j: (0, j))]
                 + [_tab_spec(t, tm) for t in tabs],
        out_specs=pl.BlockSpec((tm, RET_COLS), lambda i, j: (i, j)),
        out_shape=jax.ShapeDtypeStruct((n, f), F32),
        compiler_params=_cp(("parallel", "arbitrary"), BIG_VMEM_LIMIT),
        name="ret_proj",
    )(x, sh, sc, w_in, *tabs)


def _ret_decay_tables():
    ln = RET_CHUNK
    lg = jnp.log1p(-jnp.exp2(-5.0 - jnp.arange(RET_HEADS, dtype=F32)))
    idx = jnp.arange(ln, dtype=F32)
    diff = idx[:, None] - idx[None, :]
    decay = jnp.where(diff >= 0, jnp.exp(jnp.maximum(diff, 0.0)[None] * lg[:, None, None]), 0.0)
    rs = jnp.exp((idx[None, :] + 1.0) * lg[:, None])[:, :, None]
    wk = jnp.exp((ln - 1.0 - idx)[None, :] * lg[:, None])[:, :, None]
    gl = jnp.exp(ln * lg)[:, None, None]
    g1 = jnp.exp(lg)[:, None, None]
    return decay, rs, wk, gl, g1


def _ret_chunk_kernel(q_ref, k_ref, v_ref, dec_ref, rs_ref, wk_ref, gl_ref, o_ref, sf_ref, s_sc):
    c = pl.program_id(2)

    @pl.when(c == 0)
    def _():
        s_sc[...] = jnp.zeros_like(s_sc)

    q = q_ref[...].astype(BF16)
    k = k_ref[...]
    v = v_ref[...].astype(BF16)
    s_old = s_sc[...]
    inner = lax.dot_general(q, k.astype(BF16), _NT, preferred_element_type=F32) * dec_ref[0]
    o = jnp.dot(inner.astype(BF16), v, preferred_element_type=F32)
    o = o + jnp.dot(q, s_old.astype(BF16), preferred_element_type=F32) * rs_ref[0]
    o_ref[...] = o
    kw_t = (k * wk_ref[0]).T.astype(BF16)
    s_new = gl_ref[0] * s_old + jnp.dot(kw_t, v, preferred_element_type=F32)
    s_sc[...] = s_new

    @pl.when(c == pl.num_programs(2) - 1)
    def _():
        sf_ref[0, 0] = s_new


def _ret_chunks(z, b, s, tables):
    decay, rs, wk, gl, _ = tables
    ln = RET_CHUNK
    nc = s // ln
    hk = RET_HEADS
    kv0 = 2 * hk * RET_DK // RET_DV
    return pl.pallas_call(
        _ret_chunk_kernel,
        grid=(b, hk, nc),
        in_specs=[pl.BlockSpec((ln, RET_DK), lambda bb, hh, c: (bb * nc + c, hh)),
                  pl.BlockSpec((ln, RET_DK), lambda bb, hh, c: (bb * nc + c, hk + hh)),
                  pl.BlockSpec((ln, RET_DV), lambda bb, hh, c: (bb * nc + c, kv0 + hh)),
                  pl.BlockSpec((1, ln, ln), lambda bb, hh, c: (hh, 0, 0)),
                  pl.BlockSpec((1, ln, 1), lambda bb, hh, c: (hh, 0, 0)),
                  pl.BlockSpec((1, ln, 1), lambda bb, hh, c: (hh, 0, 0)),
                  pl.BlockSpec((1, 1, 1), lambda bb, hh, c: (hh, 0, 0))],
        out_specs=[pl.BlockSpec((ln, RET_DV), lambda bb, hh, c: (bb * nc + c, hh)),
                   pl.BlockSpec((1, 1, RET_DK, RET_DV), lambda bb, hh, c: (bb, hh, 0, 0))],
        out_shape=[jax.ShapeDtypeStruct((b * s, hk * RET_DV), F32),
                   jax.ShapeDtypeStruct((b, hk, RET_DK, RET_DV), F32)],
        scratch_shapes=[pltpu.VMEM((RET_DK, RET_DV), F32)],
        compiler_params=_cp(("parallel", "parallel", "arbitrary")),
        name="ret_chunks",
    )(z, z, z, decay, rs, wk, gl)


def _ret_step_kernel(s_ref, q_ref, k_ref, v_ref, g1_ref, o_ref, sn_ref):
    s_old = s_ref[0, 0]
    qc = q_ref[0, 0]
    kc = k_ref[0, 0]
    v = v_ref[0, 0]
    gamma = g1_ref[0]
    inner = jnp.sum(qc * kc, axis=0, keepdims=True)
    o_ref[0, 0] = inner * v + jnp.sum(qc * s_old, axis=0, keepdims=True) * gamma
    sn_ref[0, 0] = gamma * s_old + kc * v


def _ret_step(state, z, g1):
    db = state.shape[0]
    hk = RET_HEADS
    q = z[:, :hk * RET_DK].reshape(db, hk, RET_DK, 1)
    k = z[:, hk * RET_DK:2 * hk * RET_DK].reshape(db, hk, RET_DK, 1)
    v = z[:, 2 * hk * RET_DK:2 * hk * RET_DK + hk * RET_DV].reshape(db, hk, 1, RET_DV)
    col = pl.BlockSpec((1, 1, RET_DK, 1), lambda bb, hh: (bb, hh, 0, 0))
    row = pl.BlockSpec((1, 1, 1, RET_DV), lambda bb, hh: (bb, hh, 0, 0))
    st = pl.BlockSpec((1, 1, RET_DK, RET_DV), lambda bb, hh: (bb, hh, 0, 0))
    o, s_new = pl.pallas_call(
        _ret_step_kernel,
        grid=(db, hk),
        in_specs=[st, col, col, row, pl.BlockSpec((1, 1, 1), lambda bb, hh: (hh, 0, 0))],
        out_specs=[row, st],
        out_shape=[jax.ShapeDtypeStruct((db, hk, 1, RET_DV), F32),
                   jax.ShapeDtypeStruct(state.shape, F32)],
        compiler_params=_cp(("parallel", "parallel")),
        name="ret_step",
    )(state, q, k, v, g1)
    return o.reshape(db, hk * RET_DV), s_new


def _ret_out_kernel(o_ref, gz_ref, gn_ref, wo_ref, x_ref, gt_ref, g_ref, b_ref, y_ref, t_sc, *, alpha):
    for hh in range(RET_HEADS):
        oh = o_ref[:, hh * RET_DV:(hh + 1) * RET_DV]
        mu = jnp.mean(oh, axis=-1, keepdims=True)
        oc = oh - mu
        var = jnp.mean(oc * oc, axis=-1, keepdims=True)
        yh = oc * lax.rsqrt(var + LN_EPS) * gn_ref[:, hh * RET_DV:(hh + 1) * RET_DV]
        gz = gz_ref[:, hh * RET_DV:(hh + 1) * RET_DV]
        t_sc[:, hh * RET_DV:(hh + 1) * RET_DV] = (gz * jax.nn.sigmoid(gz) * yh).astype(BF16)
    f = jnp.dot(t_sc[...], wo_ref[...], preferred_element_type=F32)
    y_ref[...] = _post_norm(x_ref[...], f, gt_ref[0], g_ref[...], b_ref[...], alpha)


def _ret_out(o, z, gn, w_o, x, gate, ln_g, ln_b, alpha, tm):
    n, d = x.shape
    hv = RET_HEADS * RET_DV
    gblk = z.shape[1] // hv - 1
    return pl.pallas_call(
        functools.partial(_ret_out_kernel, alpha=alpha),
        grid=(n // tm,),
        in_specs=[_tok_spec(tm, hv), pl.BlockSpec((tm, hv), lambda i: (i, gblk)),
                  _full_spec(gn), _full_spec(w_o), _tok_spec(tm, d), _mod_spec(gate, n, tm),
                  _full_spec(ln_g), _full_spec(ln_b)],
        out_specs=_tok_spec(tm, d),
        out_shape=jax.ShapeDtypeStruct((n, d), F32),
        scratch_shapes=[pltpu.VMEM((tm, hv), BF16)],
        compiler_params=_cp(("parallel",), BIG_VMEM_LIMIT),
        name="ret_out",
    )(o, z, gn, w_o, x, gate, ln_g, ln_b)


def _moba_proj_kernel(x_ref, sh_ref, sc_ref, w_ref, c_ref, sa_ref, sb_ref,
                      q_ref, k_ref, v_ref, kb_ref, vb_ref, km_ref):
    hd, dh = MOBA_HEADS, MOBA_HEAD_DIM
    h = (x_ref[...] * (1.0 + sc_ref[0]) + sh_ref[0]).astype(BF16)
    z = jnp.dot(h, w_ref[...], preferred_element_type=F32)
    c, sa, sb = c_ref[...], sa_ref[...], sb_ref[...]
    q = _rope(z[:, :hd * dh], c, sa, sb, MOBA_ROT_DIM // 2)
    k = _rope(z[:, hd * dh:2 * hd * dh], c, sa, sb, MOBA_ROT_DIM // 2)
    q_ref[...] = q.astype(BF16)
    for hh in range(hd):
        kh = k[:, hh * dh:(hh + 1) * dh]
        vh = z[:, (2 * hd + hh) * dh:(2 * hd + hh + 1) * dh]
        k_ref[0, hh] = kh
        v_ref[0, hh] = vh
        kb_ref[0, hh] = kh.astype(BF16)
        vb_ref[0, hh] = vh.astype(BF16)
        km_ref[0, 0, hh:hh + 1, :] = jnp.mean(kh, axis=0, keepdims=True)


def _moba_proj(x, sh, sc, w_qkv, tabs, b, s, tm):
    n, d = x.shape
    hd, dh = MOBA_HEADS, MOBA_HEAD_DIM
    nt = s // tm
    kv_spec = pl.BlockSpec((1, hd, tm, dh), lambda i: (i // nt, 0, i % nt, 0))
    kv_shape = (b, hd, s, dh)
    return pl.pallas_call(
        _moba_proj_kernel,
        grid=(n // tm,),
        in_specs=[_tok_spec(tm, d), _mod_spec(sh, n, tm), _mod_spec(sc, n, tm), _full_spec(w_qkv)]
                 + [_tab_spec(t, tm) for t in tabs],
        out_specs=[_tok_spec(tm, hd * dh), kv_spec, kv_spec, kv_spec, kv_spec,
                   pl.BlockSpec((1, 1, hd, dh), lambda i: (i // nt, i % nt, 0, 0))],
        out_shape=[jax.ShapeDtypeStruct((n, hd * dh), BF16),
                   jax.ShapeDtypeStruct(kv_shape, F32), jax.ShapeDtypeStruct(kv_shape, F32),
                   jax.ShapeDtypeStruct(kv_shape, BF16), jax.ShapeDtypeStruct(kv_shape, BF16),
                   jax.ShapeDtypeStruct((b, nt, hd, dh), F32)],
        compiler_params=_cp(("parallel",), BIG_VMEM_LIMIT),
        name="moba_proj",
    )(x, sh, sc, w_qkv, *tabs)


def _moba_attn_kernel(q_ref, k_ref, v_ref, km_ref, o_ref, m_sc, l_sc, acc_sc, sel_sc, *, nb):
    hd, dh, blk = MOBA_HEADS, MOBA_HEAD_DIM, MOBA_BLOCK
    i = pl.program_id(1)
    jj = pl.program_id(2)
    col = lax.broadcasted_iota(jnp.int32, (blk, LANES), 1)

    @pl.when(jj == 0)
    def _():
        rr = lax.broadcasted_iota(jnp.int32, (blk, blk), 0)
        cc = lax.broadcasted_iota(jnp.int32, (blk, blk), 1)
        colf = col.astype(F32)
        for hh in range(hd):
            q = q_ref[:, hh * dh:(hh + 1) * dh]
            km = km_ref[0, hh].astype(BF16)
            gate = lax.dot_general(q, km, _NT, preferred_element_type=F32)
            gate = jnp.where(col < i, gate, -jnp.inf)
            sel = jnp.zeros((blk, LANES), F32)
            for _ in range(min(MOBA_TOPK, nb)):
                mx = jnp.max(gate, axis=1, keepdims=True)
                idx = jnp.min(jnp.where(gate == mx, colf, float(LANES)), axis=1, keepdims=True)
                hit = colf == idx
                sel = jnp.where(hit & (col < i), 1.0, sel)
                gate = jnp.where(hit, -jnp.inf, gate)
            sel_sc[hh] = sel
            s = lax.dot_general(q, k_ref[0, hh], _NT, preferred_element_type=F32) * MOBA_SCALE
            s = jnp.where(cc <= rr, s, NEG)
            mx = jnp.max(s, axis=1, keepdims=True)
            p = jnp.exp(s - mx)
            m_sc[hh] = mx
            l_sc[hh] = jnp.sum(p, axis=1, keepdims=True)
            acc_sc[hh] = jnp.dot(p.astype(BF16), v_ref[0, hh], preferred_element_type=F32)

    @pl.when((jj > 0) & (jj - 1 < i))
    def _():
        n = jj - 1
        for hh in range(hd):
            q = q_ref[:, hh * dh:(hh + 1) * dh]
            picked = jnp.sum(jnp.where(col == n, sel_sc[hh], 0.0), axis=1, keepdims=True)
            s = lax.dot_general(q, k_ref[0, hh], _NT, preferred_element_type=F32) * MOBA_SCALE
            s = jnp.where(picked > 0.5, s, NEG)
            m_prev = m_sc[hh]
            m_new = jnp.maximum(m_prev, jnp.max(s, axis=1, keepdims=True))
            alpha = jnp.exp(m_prev - m_new)
            p = jnp.exp(s - m_new)
            l_sc[hh] = alpha * l_sc[hh] + jnp.sum(p, axis=1, keepdims=True)
            acc_sc[hh] = alpha * acc_sc[hh] + jnp.dot(p.astype(BF16), v_ref[0, hh],
                                                      preferred_element_type=F32)
            m_sc[hh] = m_new

    @pl.when(jj == pl.num_programs(2) - 1)
    def _():
        for hh in range(hd):
            o_ref[:, hh * dh:(hh + 1) * dh] = (acc_sc[hh] / l_sc[hh]).astype(o_ref.dtype)


def _moba_attn(q, kb, vb, kmean, b, s):
    hd, dh, blk = MOBA_HEADS, MOBA_HEAD_DIM, MOBA_BLOCK
    nb = s // blk
    assert nb <= LANES

    def kvmap(bb, i, jj):
        return (bb, 0, jnp.where(jj == 0, i, jnp.minimum(jj - 1, jnp.maximum(i - 1, 0))), 0)

    return pl.pallas_call(
        functools.partial(_moba_attn_kernel, nb=nb),
        grid=(b, nb, nb),
        in_specs=[pl.BlockSpec((blk, hd * dh), lambda bb, i, jj: (bb * nb + i, 0)),
                  pl.BlockSpec((1, hd, blk, dh), kvmap),
                  pl.BlockSpec((1, hd, blk, dh), kvmap),
                  pl.BlockSpec((1, hd, LANES, dh), lambda bb, i, jj: (bb, 0, 0, 0))],
        out_specs=pl.BlockSpec((blk, hd * dh), lambda bb, i, jj: (bb * nb + i, 0)),
        out_shape=jax.ShapeDtypeStruct((b * s, hd * dh), BF16),
        scratch_shapes=[pltpu.VMEM((hd, blk, 1), F32), pltpu.VMEM((hd, blk, 1), F32),
                        pltpu.VMEM((hd, blk, dh), F32), pltpu.VMEM((hd, blk, LANES), F32)],
        compiler_params=_cp(("parallel", "parallel", "arbitrary"), BIG_VMEM_LIMIT),
        name="moba_attn",
    )(q, kb, vb, kmean)


MOBA_DEC_PAGES = 4


def _moba_dec_mean_kernel(pt_ref, q_ref, *rest, ppb, own):
    gp = MOBA_DEC_PAGES
    pages = rest[:gp]
    sel_ref = rest[gp]
    km_sc = rest[gp + 1]
    g = pl.program_id(1)
    inv = 1.0 / (ppb * PAGE_SIZE)
    for t in range(0, gp, ppb):
        acc = jnp.sum(pages[t][0, 0], axis=1)
        for r in range(1, ppb):
            acc = acc + jnp.sum(pages[t + r][0, 0], axis=1)
        km_sc[g * (gp // ppb) + t // ppb] = acc * inv

    @pl.when(g == pl.num_programs(1) - 1)
    def _():
        gate = jnp.sum(km_sc[...] * q_ref[0][None], axis=-1)
        rowf = lax.broadcasted_iota(jnp.int32, gate.shape, 0).astype(F32)
        for t in range(min(MOBA_TOPK, own)):
            mx = jnp.max(gate, axis=0, keepdims=True)
            idx = jnp.min(jnp.where(gate == mx, rowf, float(own)), axis=0, keepdims=True)
            sel_ref[0, t:t + 1, :] = idx.astype(jnp.int32)
            gate = jnp.where(rowf == idx, -jnp.inf, gate)


def _moba_dec_select(page_table, q, pool_k, layer):
    db, n_pages = page_table.shape
    hd, dh = MOBA_HEADS, MOBA_HEAD_DIM
    ppb = MOBA_BLOCK // PAGE_SIZE
    own = n_pages // ppb
    gp = MOBA_DEC_PAGES
    assert gp % ppb == 0 and (own * ppb) % gp == 0
    topk = min(MOBA_TOPK, own)

    def page_spec(t):
        return pl.BlockSpec((1, 1, hd, PAGE_SIZE, dh),
                            lambda bb, g, pt: (layer, pt[bb, g * gp + t], 0, 0, 0))

    return pl.pallas_call(
        functools.partial(_moba_dec_mean_kernel, ppb=ppb, own=own),
        grid_spec=pltpu.PrefetchScalarGridSpec(
            num_scalar_prefetch=1,
            grid=(db, own * ppb // gp),
            in_specs=[pl.BlockSpec((1, hd, dh), lambda bb, g, pt: (bb, 0, 0))]
                     + [page_spec(t) for t in range(gp)],
            out_specs=pl.BlockSpec((1, topk, hd), lambda bb, g, pt: (bb, 0, 0)),
            scratch_shapes=[pltpu.VMEM((own, hd, dh), F32)]),
        out_shape=jax.ShapeDtypeStruct((db, topk, hd), jnp.int32),
        compiler_params=_cp(("parallel", "arbitrary")),
        name="moba_dec_select",
    )(page_table, q, *([pool_k] * gp))


def _moba_dec_attn_kernel(pt_ref, sel_ref, q_ref, kn_ref, vn_ref, *rest, npg):
    kp = rest[:npg]
    vp = rest[npg:2 * npg]
    o_ref = rest[2 * npg]
    q = q_ref[0, 0]
    s_self = jnp.sum(q * kn_ref[0, 0], axis=1, keepdims=True) * MOBA_SCALE
    ss = [jnp.sum(kp[t][0, 0, 0] * q, axis=1, keepdims=True) * MOBA_SCALE for t in range(npg)]
    mx = s_self
    for s in ss:
        mx = jnp.maximum(mx, jnp.max(s, axis=0, keepdims=True))
    p_self = jnp.exp(s_self - mx)
    den = p_self
    acc = p_self * vn_ref[0, 0]
    for t in range(npg):
        p = jnp.exp(ss[t] - mx)
        den = den + jnp.sum(p, axis=0, keepdims=True)
        acc = acc + jnp.sum(p * vp[t][0, 0, 0], axis=0, keepdims=True)
    o_ref[0, 0] = acc / den


def _moba_dec_attn(page_table, sel, q, kn, vn, pool_k, pool_v, layer):
    db = page_table.shape[0]
    hd, dh = MOBA_HEADS, MOBA_HEAD_DIM
    ppb = MOBA_BLOCK // PAGE_SIZE
    topk = sel.shape[1]
    npg = topk * ppb

    def page_spec(t):
        kk, r = t // ppb, t % ppb
        return pl.BlockSpec((1, 1, 1, PAGE_SIZE, dh),
                            lambda bb, hh, pt, sl: (layer, pt[bb, sl[bb, kk, hh] * ppb + r], hh, 0, 0))

    vec = pl.BlockSpec((1, 1, 1, dh), lambda bb, hh, pt, sl: (bb, hh, 0, 0))
    return pl.pallas_call(
        functools.partial(_moba_dec_attn_kernel, npg=npg),
        grid_spec=pltpu.PrefetchScalarGridSpec(
            num_scalar_prefetch=2,
            grid=(db, hd),
            in_specs=[vec, vec, vec] + [page_spec(t) for t in range(npg)] * 2,
            out_specs=vec),
        out_shape=jax.ShapeDtypeStruct((db, hd, 1, dh), F32),
        compiler_params=_cp(("parallel", "parallel")),
        name="moba_dec_attn",
    )(page_table, sel, q, kn, vn, *([pool_k] * npg), *([pool_v] * npg))


def _proj_out_kernel(o_ref, wo_ref, x_ref, gt_ref, g_ref, b_ref, y_ref, *, alpha):
    f = jnp.dot(o_ref[...].astype(BF16), wo_ref[...], preferred_element_type=F32)
    y_ref[...] = _post_norm(x_ref[...], f, gt_ref[0], g_ref[...], b_ref[...], alpha)


def _proj_out(o, w_o, x, gate, ln_g, ln_b, alpha, tm):
    n, d = x.shape
    return pl.pallas_call(
        functools.partial(_proj_out_kernel, alpha=alpha),
        grid=(n // tm,),
        in_specs=[_tok_spec(tm, o.shape[1]), _full_spec(w_o), _tok_spec(tm, d),
                  _mod_spec(gate, n, tm), _full_spec(ln_g), _full_spec(ln_b)],
        out_specs=_tok_spec(tm, d),
        out_shape=jax.ShapeDtypeStruct((n, d), F32),
        compiler_params=_cp(("parallel",), BIG_VMEM_LIMIT),
        name="proj_out",
    )(o, w_o, x, gate, ln_g, ln_b)


_PEER_PAIRS = [(i, j) for i in range(PEER_TOPK) for j in range(PEER_TOPK) if (i + 1) * (j + 1) <= PEER_TOPK]


def _peer_topk_kernel(x_ref, sh_ref, sc_ref, wq_ref, k1_ref, k2_ref,
                      h_ref, ca_ref, e1_ref, r2_ref, e2_ref,
                      qb_sc, s1_sc, s2_sc, r1_sc, v1_sc, v2_sc, c_sc, zi_sc, *, tm):
    nk, hd, kt = PEER_N_KEYS, PEER_HEADS, PEER_TOPK
    half = PEER_KEY_DIM // 2
    h = (x_ref[...] * (1.0 + sc_ref[0]) + sh_ref[0]).astype(BF16)
    h_ref[...] = h
    qb_sc[...] = jnp.dot(h, wq_ref[...], preferred_element_type=F32).astype(BF16)
    iota_k = lax.broadcasted_iota(jnp.int32, (nk, LANES), 0).astype(F32)
    chunks = [slice(c * LANES, (c + 1) * LANES) for c in range(tm // LANES)]

    for hh in range(hd):
        s1_sc[hh] = lax.dot_general(k1_ref[...], qb_sc[:, hh * 2 * half:hh * 2 * half + half], _NT,
                                    preferred_element_type=F32)
        s2_sc[hh] = lax.dot_general(k2_ref[...], qb_sc[:, hh * 2 * half + half:(hh + 1) * 2 * half], _NT,
                                    preferred_element_type=F32)
        for lanes in chunks:
            for s_sc, v_sc, r_out in ((s1_sc, v1_sc, r1_sc), (s2_sc, v2_sc, r2_ref)):
                def body(k, carry, v_sc=v_sc, lanes=lanes, hh=hh):
                    s, r = carry
                    m = jnp.max(s, axis=0, keepdims=True)
                    idx = jnp.min(jnp.where(s == m, iota_k, float(nk)), axis=0, keepdims=True)
                    hit = iota_k == idx
                    v_sc[k, hh:hh + 1, lanes] = m
                    return jnp.where(hit, -jnp.inf, s), jnp.where(hit, k.astype(F32), r)

                _, r = lax.fori_loop(0, kt, body, (s_sc[hh, :, lanes], jnp.full((nk, LANES), float(kt), F32)))
                r_out[hh, :, lanes] = r

    pos = [float(i * kt + j) for i, j in _PEER_PAIRS]
    for lanes in chunks:
        v1 = [v1_sc[i, :, lanes] for i in range(kt)]
        v2 = [v2_sc[j, :, lanes] for j in range(kt)]
        cand0 = tuple(v1[i] + v2[j] for i, j in _PEER_PAIRS)
        m0 = cand0[0]
        zero = jnp.zeros((hd, LANES), F32)

        def cbody(_, carry, m0=m0):
            cand, cnt, z = carry
            m = cand[0]
            for cv in cand[1:]:
                m = jnp.maximum(m, cv)
            pmin = jnp.full((hd, LANES), 1e9, F32)
            for cv, pp in zip(cand, pos):
                pmin = jnp.minimum(pmin, jnp.where(cv == m, pp, 1e9))
            new_cand = []
            new_cnt = list(cnt)
            for (ci, _cj), cv, pp in zip(_PEER_PAIRS, cand, pos):
                hit = pmin == pp
                new_cand.append(jnp.where(hit, -jnp.inf, cv))
                new_cnt[ci] = new_cnt[ci] + jnp.where(hit, 1.0, 0.0)
            return tuple(new_cand), tuple(new_cnt), z + jnp.exp(m - m0)

        _, cnt, z = lax.fori_loop(0, kt, cbody, (cand0, (zero,) * kt, zero))
        for i in range(kt):
            c_sc[i, :, lanes] = cnt[i]
        zi_sc[:, lanes] = 1.0 / z

    for hh in range(hd):
        for lanes in chunks:
            r1 = r1_sc[hh, :, lanes]
            ca = jnp.zeros((nk, LANES), F32)
            for i in range(kt):
                ca = jnp.where(r1 == float(i), c_sc[i, hh:hh + 1, lanes], ca)
            ca_ref[hh, :, lanes] = ca
            e1_ref[hh, :, lanes] = (jnp.exp(s1_sc[hh, :, lanes] - v1_sc[0, hh:hh + 1, lanes])
                                    * zi_sc[hh:hh + 1, lanes])
            e2_ref[hh, :, lanes] = jnp.exp(s2_sc[hh, :, lanes] - v2_sc[0, hh:hh + 1, lanes])


def _peer_topk(x, sh, sc, w_q, k1, k2, tm):
    n, d = x.shape
    nk, hd, kt = PEER_N_KEYS, PEER_HEADS, PEER_TOPK
    sel_spec = pl.BlockSpec((hd, nk, tm), lambda i: (0, 0, i))
    sel_shape = jax.ShapeDtypeStruct((hd, nk, n), F32)
    return pl.pallas_call(
        functools.partial(_peer_topk_kernel, tm=tm),
        grid=(n // tm,),
        in_specs=[_tok_spec(tm, d), _mod_spec(sh, n, tm), _mod_spec(sc, n, tm),
                  _full_spec(w_q), _full_spec(k1), _full_spec(k2)],
        out_specs=[_tok_spec(tm, d), sel_spec, sel_spec, sel_spec, sel_spec],
        out_shape=[jax.ShapeDtypeStruct((n, d), BF16), sel_shape, sel_shape, sel_shape, sel_shape],
        scratch_shapes=[pltpu.VMEM((tm, hd * PEER_KEY_DIM), BF16),
                        pltpu.VMEM((hd, nk, tm), F32), pltpu.VMEM((hd, nk, tm), F32),
                        pltpu.VMEM((hd, nk, tm), F32),
                        pltpu.VMEM((kt, hd, tm), F32), pltpu.VMEM((kt, hd, tm), F32),
                        pltpu.VMEM((kt, hd, tm), F32), pltpu.VMEM((hd, tm), F32)],
        compiler_params=_cp(("parallel",), BIG_VMEM_LIMIT),
        name="peer_topk",
    )(x, sh, sc, w_q, k1, k2)


def _peer_dense_kernel(h_ref, u_ref, vt_ref, ca_ref, e1_ref, r2_ref, e2_ref,
                       x_ref, gt_ref, g_ref, b_ref, y_ref, a_sc, p_sc, acc_sc, *, na, tm, alpha):
    nk, hd = PEER_N_KEYS, PEER_HEADS
    j = pl.program_id(1)

    @pl.when(j == 0)
    def _():
        acc_sc[...] = jnp.zeros_like(acc_sc)

    hu = lax.dot_general(u_ref[...], h_ref[...], _NT, preferred_element_type=F32)
    a_sc[...] = 0.5 * hu * (1.0 + lax.erf(hu * INV_SQRT2))
    for ai in range(na):
        rows = slice(ai * nk, (ai + 1) * nk)
        for c in range(tm // LANES):
            lanes = slice(c * LANES, (c + 1) * LANES)
            w = jnp.zeros((nk, LANES), F32)
            for hh in range(hd):
                ca = ca_ref[hh, ai, :, lanes]
                e1 = e1_ref[hh, ai, :, lanes]
                w = w + jnp.where(r2_ref[hh, :, lanes] < ca, e2_ref[hh, :, lanes] * e1, 0.0)
            p_sc[rows, lanes] = (w * a_sc[rows, lanes]).astype(BF16)
    acc_sc[...] += jnp.dot(vt_ref[...], p_sc[...], preferred_element_type=F32)

    @pl.when(j == pl.num_programs(1) - 1)
    def _():
        f = acc_sc[...].T
        y_ref[...] = _post_norm(x_ref[...], f, gt_ref[0], g_ref[...], b_ref[...], alpha)


def _peer_dense(h, u, vt, sel, x, gate, ln_g, ln_b, alpha, tm, na):
    n, d = x.shape
    nk, hd = PEER_N_KEYS, PEER_HEADS
    te = na * nk
    ne = u.shape[0]
    sel_spec = pl.BlockSpec((hd, nk, tm), lambda i, j: (0, 0, i))
    row_spec = pl.BlockSpec((hd, na, 1, tm), lambda i, j: (0, j, 0, i))
    ca, e1, r2, e2 = sel
    ca = ca.reshape(hd, nk, 1, n)
    e1 = e1.reshape(hd, nk, 1, n)
    return pl.pallas_call(
        functools.partial(_peer_dense_kernel, na=na, tm=tm, alpha=alpha),
        grid=(n // tm, ne // te),
        in_specs=[_tok_spec(tm, d), pl.BlockSpec((te, d), lambda i, j: (j, 0)),
                  pl.BlockSpec((d, te), lambda i, j: (0, j)),
                  row_spec, row_spec, sel_spec, sel_spec,
                  _tok_spec(tm, d), _mod_spec(gate, n, tm), _full_spec(ln_g), _full_spec(ln_b)],
        out_specs=_tok_spec(tm, d),
        out_shape=jax.ShapeDtypeStruct((n, d), F32),
        scratch_shapes=[pltpu.VMEM((te, tm), F32), pltpu.VMEM((te, tm), BF16), pltpu.VMEM((d, tm), F32)],
        compiler_params=_cp(("parallel", "arbitrary"), BIG_VMEM_LIMIT),
        name="peer_dense",
    )(h, u, vt, ca, e1, r2, e2, x, gate, ln_g, ln_b)


def _mods(m, ls, b, db, d):
    mp = m[ls, :b]
    ms = m[ls, b:b + db]
    prompt = tuple(mp[:, None, k * d:(k + 1) * d] for k in range(3))
    sample = tuple(ms[None, :, k * d:(k + 1) * d] for k in range(3))
    return prompt, sample


def kernel(x_prompt, x_sample, cache_mla_ckv, cache_mla_kpe, cache_moba_k, cache_moba_v, state_ret, page_table, c_prompt, c_sample, ada_w, ada_b, ln_g, ln_b, mla_w_in, mla_q_norm, mla_kv_norm, mla_w_uq, mla_w_uk, mla_w_uv, mla_w_o, ret_w_in, ret_gn, ret_w_o, moba_w_qkv, moba_w_o, peer_w_q, peer_k1, peer_k2, peer_u, peer_v):
    b, s, d = x_prompt.shape
    db, t_new, _ = x_sample.shape
    assert t_new == 1
    depth = ada_w.shape[0]
    alpha = (2 * depth) ** 0.25
    n_pages = page_table.shape[1]
    past = n_pages * PAGE_SIZE
    n_p = b * s
    tm_p = min(256, s)
    tm_s = db
    assert s % MOBA_BLOCK == 0 and tm_p == MOBA_BLOCK and db % 16 == 0

    pad = (-(b + db)) % 8
    c_all = jnp.concatenate([c_prompt, c_sample, jnp.zeros((pad, d), F32)], axis=0)
    mod = _adaln(c_all, ada_w, ada_b)

    pos_p = jnp.arange(s)
    pos_s = jnp.full((1,), past)
    ret_tabs = _ret_decay_tables()

    xp = x_prompt.reshape(n_p, d)
    xs = x_sample.reshape(db, d)
    outs = {k: [] for k in ("ckv_p", "kpe_p", "ckv_s", "kpe_s", "mk_p", "mv_p", "mk_s", "mv_s", "rs_p", "rs_s")}

    for i in range(depth):
        j = i // N_MIXERS
        kind = i % N_MIXERS
        (shp, scp, gtp), (shs, scs, gts) = _mods(mod, 2 * i, b, db, d)
        g0, b0 = ln_g[i, 0][None], ln_b[i, 0][None]
        if kind == 0:
            hd = MLA_HEADS
            w_in = jnp.pad(mla_w_in[j], ((0, 0), (0, LANES - MLA_ROPE))).astype(BF16)
            w_uq = mla_w_uq[j].reshape(MLA_Q_RANK, hd, MLA_NOPE + MLA_ROPE)
            w_uq = jnp.concatenate([w_uq[:, :, :MLA_NOPE].reshape(MLA_Q_RANK, hd * MLA_NOPE),
                                    w_uq[:, :, MLA_NOPE:].reshape(MLA_Q_RANK, hd * MLA_ROPE)], axis=1)
            w = dict(w_in=w_in, q_norm=mla_q_norm[j][None], kv_norm=mla_kv_norm[j][None],
                     w_uq=w_uq.astype(BF16),
                     w_uk=jnp.transpose(mla_w_uk[j], (1, 2, 0)).astype(BF16),
                     w_uv=jnp.transpose(mla_w_uv[j], (1, 0, 2)).astype(BF16),
                     w_o=mla_w_o[j].astype(BF16))
            for x, sh, sc, gt, pos, tm, tag in ((xp, shp, scp, gtp, pos_p, tm_p, "p"), (xs, shs, scs, gts, pos_s, tm_s, "s")):
                tq = _rope_tables(pos, MLA_ROPE_THETA, MLA_ROPE, MLA_ROPE, hd)
                tk = _rope_tables(pos, MLA_ROPE_THETA, MLA_ROPE, LANES, 1)
                ckv, kpe, ckvb, kpeb, ql, qp = _mla_proj(x, sh, sc, w, tq, tk, tm)
                if tag == "p":
                    o = _mla_attn(ql.reshape(b, s * hd, MLA_KV_RANK), qp.reshape(b, s * hd, MLA_ROPE),
                                  ckvb.reshape(b, s, MLA_KV_RANK), kpeb.reshape(b, s, MLA_ROPE), b, s)
                    o = o.reshape(n_p, hd * MLA_KV_RANK)
                    outs["ckv_p"].append(ckv.reshape(b, s, MLA_KV_RANK))
                    outs["kpe_p"].append(kpe.reshape(b, s, MLA_ROPE))
                else:
                    rpad = ((0, 0), (0, MLA_DEC_ROWS - hd), (0, 0))
                    o = _mla_dec(page_table,
                                 jnp.pad(ql.reshape(db, hd, MLA_KV_RANK), rpad),
                                 jnp.pad(qp.reshape(db, hd, MLA_ROPE), rpad),
                                 ckv[:, None, :], kpe[:, None, :], cache_mla_ckv, cache_mla_kpe, j)
                    o = o[:, :hd].reshape(db, hd * MLA_KV_RANK)
                    outs["ckv_s"].append(ckv.reshape(db, 1, MLA_KV_RANK))
                    outs["kpe_s"].append(kpe.reshape(db, 1, MLA_ROPE))
                y = _mla_out(o, w, x, gt, g0, b0, alpha, tm)
                if tag == "p":
                    xp = y
                else:
                    xs = y
        elif kind == 1:
            w_in = ret_w_in[j].astype(BF16)
            w_o = ret_w_o[j].astype(BF16)
            gn = ret_gn[j][None]
            tp = _rope_tables(pos_p, RET_ROPE_THETA, RET_DK, RET_DK, RET_HEADS)
            ts = _rope_tables(pos_s, RET_ROPE_THETA, RET_DK, RET_DK, RET_HEADS)
            zp = _ret_proj(xp, shp, scp, w_in, tp, tm_p)
            op, sfin = _ret_chunks(zp, b, s, ret_tabs)
            xp = _ret_out(op, zp, gn, w_o, xp, gtp, g0, b0, alpha, tm_p)
            zs = _ret_proj(xs, shs, scs, w_in, ts, tm_s)
            os_, snew = _ret_step(state_ret[j], zs, ret_tabs[4])
            xs = _ret_out(os_, zs, gn, w_o, xs, gts, g0, b0, alpha, tm_s)
            outs["rs_p"].append(sfin)
            outs["rs_s"].append(snew)
        else:
            hd, dh = MOBA_HEADS, MOBA_HEAD_DIM
            w_qkv = moba_w_qkv[j].astype(BF16)
            w_o = moba_w_o[j].astype(BF16)
            tp = _rope_tables(pos_p, MOBA_ROPE_THETA, MOBA_ROT_DIM, dh, hd)
            ts = _rope_tables(pos_s, MOBA_ROPE_THETA, MOBA_ROT_DIM, dh, hd)
            q, k, v, kb, vb, kmean = _moba_proj(xp, shp, scp, w_qkv, tp, b, s, tm_p)
            nb = s // MOBA_BLOCK
            kmean = jnp.pad(jnp.transpose(kmean, (0, 2, 1, 3)), ((0, 0), (0, 0), (0, LANES - nb), (0, 0)))
            op = _moba_attn(q, kb, vb, kmean, b, s)
            xp = _proj_out(op, w_o, xp, gtp, g0, b0, alpha, tm_p)
            outs["mk_p"].append(k)
            outs["mv_p"].append(v)
            qs, ks, vs, _, _, _ = _moba_proj(xs, shs, scs, w_qkv, ts, 1, db, tm_s)
            qs = qs.astype(F32).reshape(db, hd, dh)
            ks = jnp.transpose(ks[0], (1, 0, 2))[:, :, None, :]
            vs = jnp.transpose(vs[0], (1, 0, 2))[:, :, None, :]
            sel = _moba_dec_select(page_table, qs, cache_moba_k, j)
            os_ = _moba_dec_attn(page_table, sel, qs[:, :, None, :], ks, vs, cache_moba_k, cache_moba_v, j)
            xs = _proj_out(os_.reshape(db, hd * dh), w_o, xs, gts, g0, b0, alpha, tm_s)
            outs["mk_s"].append(ks)
            outs["mv_s"].append(vs)

        (shp, scp, gtp), (shs, scs, gts) = _mods(mod, 2 * i + 1, b, db, d)
        g1, b1 = ln_g[i, 1][None], ln_b[i, 1][None]
        w_q = peer_w_q[i].astype(BF16)
        k1 = peer_k1[i].astype(BF16)
        k2 = peer_k2[i].astype(BF16)
        u = peer_u[i].astype(BF16)
        vt = peer_v[i].T.astype(BF16)
        hp, *selp = _peer_topk(xp, shp, scp, w_q, k1, k2, tm_p)
        xp = _peer_dense(hp, u, vt, selp, xp, gtp, g1, b1, alpha, tm_p, 4)
        hs, *sels = _peer_topk(xs, shs, scs, w_q, k1, k2, tm_s)
        xs = _peer_dense(hs, u, vt, sels, xs, gts, g1, b1, alpha, tm_s, 4)

    return (xp.reshape(b, s, d), xs.reshape(db, 1, d),
            jnp.stack(outs["ckv_p"]), jnp.stack(outs["kpe_p"]), jnp.stack(outs["ckv_s"]), jnp.stack(outs["kpe_s"]),
            jnp.stack(outs["mk_p"]), jnp.stack(outs["mv_p"]), jnp.stack(outs["mk_s"]), jnp.stack(outs["mv_s"]),
            jnp.stack(outs["rs_p"]), jnp.stack(outs["rs_s"]))
```

```python
import functools
import math

import jax
import jax.numpy as jnp
from jax import lax
from jax.experimental import pallas as pl
from jax.experimental.pallas import tpu as pltpu

F32 = jnp.float32
BF16 = jnp.bfloat16

N_MIXERS = 3
LN_EPS = 1e-5
RMS_EPS = 1e-6
PAGE_SIZE = 128

MLA_HEADS = 8
MLA_NOPE = 128
MLA_ROPE = 64
MLA_VDIM = 128
MLA_KV_RANK = 256
MLA_Q_RANK = 512
MLA_ROPE_THETA = 10000.0
MLA_SCALE = (MLA_NOPE + MLA_ROPE) ** -0.5
MLA_QK_WIDTH = MLA_KV_RANK + 128

RET_HEADS = 4
RET_DK = 256
RET_DV = 512
RET_CHUNK = 128
RET_ROPE_THETA = 10000.0

MOBA_HEADS = 8
MOBA_HEAD_DIM = 128
MOBA_BLOCK = 256
MOBA_TOPK = 3
MOBA_ROPE_THETA = 500000.0
MOBA_ROT_DIM = MOBA_HEAD_DIM // 4
MOBA_SCALE = MOBA_HEAD_DIM ** -0.5

PEER_HEADS = 8
PEER_N_KEYS = 128
PEER_KEY_DIM = 256
PEER_TOPK = 16

LANES = 128
V7X_VMEM_BYTES = 64 * 1024 * 1024
BIG_VMEM_LIMIT = V7X_VMEM_BYTES * 3 // 4
NEG = -1e30
INV_SQRT2 = 0.7071067811865476

_NT = (((1,), (1,)), ((), ()))


def _cp(sem, vmem=None):
    kw = dict(dimension_semantics=sem)
    if vmem is not None:
        kw["vmem_limit_bytes"] = vmem
    return pltpu.CompilerParams(**kw)


def _tok_spec(tm, f):
    return pl.BlockSpec((tm, f), lambda i, *_: (i, 0))


def _mod_spec(mod, n, tm):
    g, r, d = mod.shape
    tiles_per_group = n // (g * tm)
    return pl.BlockSpec((1, r, d), lambda i, *_: (i // tiles_per_group, 0, 0))


def _tab_spec(tab, tm):
    p, w = tab.shape
    if p == 1:
        return pl.BlockSpec((1, w), lambda i, *_: (0, 0))
    nt = p // tm
    return pl.BlockSpec((tm, w), lambda i, *_: (i % nt, 0))


def _full_spec(a):
    nd = a.ndim
    return pl.BlockSpec(a.shape, lambda *_: (0,) * nd)


def _rope_tables(pos, theta, rot_dim, period, reps):
    half = rot_dim // 2
    freqs = jnp.exp(-math.log(theta) * jnp.arange(half, dtype=F32) * (2.0 / rot_dim))
    ang = pos.astype(F32)[:, None] * freqs[None, :]
    cos, sin = jnp.cos(ang), jnp.sin(ang)
    p = pos.shape[0]
    rest = period - rot_dim
    c = jnp.concatenate([cos, cos, jnp.ones((p, rest), F32)], axis=1)
    sa = jnp.concatenate([-sin, jnp.zeros((p, period - half), F32)], axis=1)
    sb = jnp.concatenate([jnp.zeros((p, half), F32), sin, jnp.zeros((p, rest), F32)], axis=1)
    return tuple(jnp.tile(t, (1, reps)) for t in (c, sa, sb))


def _rope(x, c, sa, sb, half):
    w = x.shape[-1]
    return x * c + pltpu.roll(x, w - half, 1) * sa + pltpu.roll(x, half, 1) * sb


def _post_norm(x, f, gate, g, b, alpha):
    y = alpha * x + (1.0 + gate) * f
    mu = jnp.mean(y, axis=-1, keepdims=True)
    yc = y - mu
    var = jnp.mean(yc * yc, axis=-1, keepdims=True)
    return yc * lax.rsqrt(var + LN_EPS) * g + b


def _rms(x):
    return x * lax.rsqrt(jnp.mean(x * x, axis=-1, keepdims=True) + RMS_EPS)


def _adaln_kernel(c_ref, w_ref, b_ref, o_ref):
    c = c_ref[...]
    a = (c * jax.nn.sigmoid(c)).astype(BF16)
    o_ref[0] = jnp.dot(a, w_ref[0].astype(BF16), preferred_element_type=F32) + b_ref[0]


def _adaln(c_all, ada_w, ada_b):
    nl = ada_w.shape[0] * ada_w.shape[1]
    d, f = ada_w.shape[2], ada_w.shape[3]
    w = ada_w.reshape(nl, d, f)
    b = ada_b.reshape(nl, 1, f)
    r = c_all.shape[0]
    tn = 1024
    return pl.pallas_call(
        _adaln_kernel,
        grid=(nl, f // tn),
        in_specs=[pl.BlockSpec((r, d), lambda l, j: (0, 0)),
                  pl.BlockSpec((1, d, tn), lambda l, j: (l, 0, j)),
                  pl.BlockSpec((1, 1, tn), lambda l, j: (l, 0, j))],
        out_specs=pl.BlockSpec((1, r, tn), lambda l, j: (l, 0, j)),
        out_shape=jax.ShapeDtypeStruct((nl, r, f), F32),
        compiler_params=_cp(("parallel", "parallel")),
        name="adaln",
    )(c_all, w, b)


def _mla_proj_kernel(x_ref, sh_ref, sc_ref, win_ref, qn_ref, kvn_ref, wuq_ref, wuk_ref,
                     cq_ref, sqa_ref, sqb_ref, ck_ref, ska_ref, skb_ref,
                     ckv_ref, kpe_ref, kcat_ref, vt_ref, qcat_ref):
    h = (x_ref[...] * (1.0 + sc_ref[0]) + sh_ref[0]).astype(BF16)
    z = jnp.dot(h, win_ref[...], preferred_element_type=F32)
    cq = _rms(z[:, :MLA_Q_RANK]) * qn_ref[...]
    ckv = _rms(z[:, MLA_Q_RANK:MLA_Q_RANK + MLA_KV_RANK]) * kvn_ref[...]
    kx = z[:, MLA_Q_RANK + MLA_KV_RANK:]
    kpe = _rope(kx, ck_ref[...], ska_ref[...], skb_ref[...], MLA_ROPE // 2)
    ckv_ref[...] = ckv
    kpe_ref[...] = kpe[:, :MLA_ROPE]
    kcat_ref[:, :MLA_KV_RANK] = ckv.astype(BF16)
    kcat_ref[:, MLA_KV_RANK:] = kpe.astype(BF16)
    vt_ref[0] = ckv.T.astype(BF16)
    q = jnp.dot(cq.astype(BF16), wuq_ref[...], preferred_element_type=F32)
    qx = q[:, MLA_HEADS * MLA_NOPE:]
    qp = _rope(qx, cq_ref[...], sqa_ref[...], sqb_ref[...], MLA_ROPE // 2) * MLA_SCALE
    zero = jnp.zeros((qp.shape[0], LANES - MLA_ROPE), F32)
    for hh in range(MLA_HEADS):
        qn = q[:, hh * MLA_NOPE:(hh + 1) * MLA_NOPE].astype(BF16)
        ql = jnp.dot(qn, wuk_ref[hh], preferred_element_type=F32) * MLA_SCALE
        base = hh * MLA_QK_WIDTH
        qcat_ref[:, base:base + MLA_KV_RANK] = ql.astype(BF16)
        qph = jnp.concatenate([qp[:, hh * MLA_ROPE:(hh + 1) * MLA_ROPE], zero], axis=1)
        qcat_ref[:, base + MLA_KV_RANK:base + MLA_QK_WIDTH] = qph.astype(BF16)


def _mla_proj(x, sh, sc, w, tabs_q, tabs_k, b, s, tm):
    n, d = x.shape
    hd = MLA_HEADS
    nt = s // tm
    ins = [x, sh, sc, w["w_in"], w["q_norm"], w["kv_norm"], w["w_uq"], w["w_uk"], *tabs_q, *tabs_k]
    in_specs = ([_tok_spec(tm, d), _mod_spec(sh, n, tm), _mod_spec(sc, n, tm)]
                + [_full_spec(a) for a in ins[3:8]]
                + [_tab_spec(t, tm) for t in ins[8:]])
    return pl.pallas_call(
        _mla_proj_kernel,
        grid=(n // tm,),
        in_specs=in_specs,
        out_specs=[_tok_spec(tm, MLA_KV_RANK), _tok_spec(tm, MLA_ROPE), _tok_spec(tm, MLA_QK_WIDTH),
                   pl.BlockSpec((1, MLA_KV_RANK, tm), lambda i: (i // nt, 0, i % nt)),
                   _tok_spec(tm, hd * MLA_QK_WIDTH)],
        out_shape=[jax.ShapeDtypeStruct((n, MLA_KV_RANK), F32), jax.ShapeDtypeStruct((n, MLA_ROPE), F32),
                   jax.ShapeDtypeStruct((n, MLA_QK_WIDTH), BF16),
                   jax.ShapeDtypeStruct((b, MLA_KV_RANK, s), BF16),
                   jax.ShapeDtypeStruct((n, hd * MLA_QK_WIDTH), BF16)],
        compiler_params=_cp(("parallel",), BIG_VMEM_LIMIT),
        name="mla_proj",
    )(*ins)


def _fold_causal(p, t, nq, count):
    n_lo = count(p)
    is_lo = t < n_lo
    return jnp.where(is_lo, p, nq - 1 - p), jnp.where(is_lo, t, t - n_lo)


def _mla_attn_kernel(q_ref, k_ref, vt_ref, o_ref, m_sc, l_sc, acc_sc, *, tq, tk, nq):
    i, j = _fold_causal(pl.program_id(1), pl.program_id(2), nq, lambda q: (q * tq + tq - 1) // tk + 1)
    hd, r, w = MLA_HEADS, MLA_KV_RANK, MLA_QK_WIDTH

    @pl.when(j == 0)
    def _():
        m_sc[...] = jnp.full((hd, 1, tq), NEG, F32)
        l_sc[...] = jnp.zeros((hd, 1, tq), F32)
        acc_sc[...] = jnp.zeros((hd, r, tq), F32)

    def step(masked):
        k = k_ref[...]
        vt = vt_ref[0]
        if masked:
            kpos = j * tk + lax.broadcasted_iota(jnp.int32, (tk, tq), 0)
            qpos = i * tq + lax.broadcasted_iota(jnp.int32, (tk, tq), 1)
            keep = kpos <= qpos
        for hh in range(hd):
            st = lax.dot_general(k, q_ref[:, hh * w:(hh + 1) * w], _NT, preferred_element_type=F32)
            if masked:
                st = jnp.where(keep, st, NEG)
            m_prev = m_sc[hh]
            m_new = jnp.maximum(m_prev, jnp.max(st, axis=0, keepdims=True))
            alpha = jnp.exp(m_prev - m_new)
            p = jnp.exp(st - m_new)
            l_sc[hh] = alpha * l_sc[hh] + jnp.sum(p, axis=0, keepdims=True)
            acc_sc[hh] = alpha * acc_sc[hh] + jnp.dot(vt, p.astype(BF16), preferred_element_type=F32)
            m_sc[hh] = m_new

    crosses_diagonal = j * tk + tk - 1 > i * tq

    @pl.when(crosses_diagonal)
    def _():
        step(True)

    @pl.when(jnp.logical_not(crosses_diagonal))
    def _():
        step(False)

    @pl.when(j == (i * tq + tq - 1) // tk)
    def _():
        for hh in range(hd):
            o_ref[:, hh * r:(hh + 1) * r] = (acc_sc[hh] / l_sc[hh]).T.astype(o_ref.dtype)


def _mla_attn(q, kcat, vt, b, s):
    hd, r, w = MLA_HEADS, MLA_KV_RANK, MLA_QK_WIDTH
    tq = min(256, s)
    tk = min(512, s)
    nq, nk = s // tq, s // tk

    def count(i):
        return (i * tq + tq - 1) // tk + 1

    steps = count(0) + count(nq - 1)
    assert nq % 2 == 0 and all(count(p) + count(nq - 1 - p) == steps for p in range(nq // 2))

    def qmap(bb, p, t):
        return (bb * nq + _fold_causal(p, t, nq, count)[0], 0)

    return pl.pallas_call(
        functools.partial(_mla_attn_kernel, tq=tq, tk=tk, nq=nq),
        grid=(b, nq // 2, steps),
        in_specs=[pl.BlockSpec((tq, hd * w), qmap),
                  pl.BlockSpec((tk, w), lambda bb, p, t: (bb * nk + _fold_causal(p, t, nq, count)[1], 0)),
                  pl.BlockSpec((1, r, tk), lambda bb, p, t: (bb, 0, _fold_causal(p, t, nq, count)[1]))],
        out_specs=pl.BlockSpec((tq, hd * r), qmap),
        out_shape=jax.ShapeDtypeStruct((b * s, hd * r), BF16),
        scratch_shapes=[pltpu.VMEM((hd, 1, tq), F32), pltpu.VMEM((hd, 1, tq), F32),
                        pltpu.VMEM((hd, r, tq), F32)],
        compiler_params=_cp(("parallel", "parallel", "arbitrary"), BIG_VMEM_LIMIT),
        name="mla_attn",
    )(q, kcat, vt)


MLA_DEC_ROWS = 16
MLA_DEC_PAGES = 16


def _mla_dec_kernel(pt_ref, ql_ref, qp_ref, kn_ref, kpn_ref, *rest, g_pages):
    pages = rest[:g_pages]
    ppages = rest[g_pages:2 * g_pages]
    o_ref = rest[2 * g_pages]
    m_sc, l_sc, acc_sc = rest[2 * g_pages + 1:]
    g = pl.program_id(1)
    ql = ql_ref[0]
    qp = qp_ref[0]

    @pl.when(g == 0)
    def _():
        kn = kn_ref[0]
        s_self = (jnp.sum(ql.astype(F32) * kn, axis=1, keepdims=True)
                  + jnp.sum(qp.astype(F32) * kpn_ref[0], axis=1, keepdims=True))
        m_sc[...] = s_self
        l_sc[...] = jnp.ones_like(s_self)
        acc_sc[...] = jnp.broadcast_to(kn, acc_sc.shape)

    kbs = []
    ss = []
    for t in range(g_pages):
        kb = pages[t][0, 0].astype(BF16)
        kpt = ppages[t][0, 0].astype(BF16)
        kbs.append(kb)
        ss.append(lax.dot_general(ql, kb, _NT, preferred_element_type=F32)
                  + jnp.dot(qp, kpt, preferred_element_type=F32))
    s = jnp.concatenate(ss, axis=1)
    m_prev = m_sc[...]
    m_new = jnp.maximum(m_prev, jnp.max(s, axis=1, keepdims=True))
    alpha = jnp.exp(m_prev - m_new)
    p = jnp.exp(s - m_new)
    l_sc[...] = alpha * l_sc[...] + jnp.sum(p, axis=1, keepdims=True)
    acc = alpha * acc_sc[...]
    for t in range(g_pages):
        acc = acc + jnp.dot(p[:, t * PAGE_SIZE:(t + 1) * PAGE_SIZE].astype(BF16), kbs[t],
                            preferred_element_type=F32)
    acc_sc[...] = acc
    m_sc[...] = m_new

    @pl.when(g == pl.num_programs(1) - 1)
    def _():
        o_ref[0] = (acc_sc[...] / l_sc[...]).astype(o_ref.dtype)


def _mla_dec(page_table, ql, qp, kn, kpn, cache_ckv, cache_kpe_t, layer):
    db, n_pages = page_table.shape
    gp = math.gcd(n_pages, MLA_DEC_PAGES)
    rows = MLA_DEC_ROWS

    def page_spec(shape, t):
        return pl.BlockSpec((1, 1) + shape, lambda bb, g, pt: (layer, pt[bb, g * gp + t], 0, 0))

    in_specs = ([pl.BlockSpec((1, rows, MLA_KV_RANK), lambda bb, g, pt: (bb, 0, 0)),
                 pl.BlockSpec((1, rows, MLA_ROPE), lambda bb, g, pt: (bb, 0, 0)),
                 pl.BlockSpec((1, 1, MLA_KV_RANK), lambda bb, g, pt: (bb, 0, 0)),
                 pl.BlockSpec((1, 1, MLA_ROPE), lambda bb, g, pt: (bb, 0, 0))]
                + [page_spec((PAGE_SIZE, MLA_KV_RANK), t) for t in range(gp)]
                + [page_spec((MLA_ROPE, PAGE_SIZE), t) for t in range(gp)])
    return pl.pallas_call(
        functools.partial(_mla_dec_kernel, g_pages=gp),
        grid_spec=pltpu.PrefetchScalarGridSpec(
            num_scalar_prefetch=1,
            grid=(db, n_pages // gp),
            in_specs=in_specs,
            out_specs=pl.BlockSpec((1, rows, MLA_KV_RANK), lambda bb, g, pt: (bb, 0, 0)),
            scratch_shapes=[pltpu.VMEM((rows, 1), F32), pltpu.VMEM((rows, 1), F32),
                            pltpu.VMEM((rows, MLA_KV_RANK), F32)]),
        out_shape=jax.ShapeDtypeStruct((db, rows, MLA_KV_RANK), BF16),
        compiler_params=_cp(("parallel", "arbitrary")),
        name="mla_dec",
    )(page_table, ql, qp, kn, kpn, *([cache_ckv] * gp), *([cache_kpe_t] * gp))


def _mla_out_kernel(o_ref, wuv_ref, wo_ref, x_ref, gt_ref, g_ref, b_ref, y_ref, t_sc, *, alpha):
    for hh in range(MLA_HEADS):
        t = jnp.dot(o_ref[:, hh * MLA_KV_RANK:(hh + 1) * MLA_KV_RANK], wuv_ref[hh],
                    preferred_element_type=F32)
        t_sc[:, hh * MLA_VDIM:(hh + 1) * MLA_VDIM] = t.astype(BF16)
    f = jnp.dot(t_sc[...], wo_ref[...], preferred_element_type=F32)
    y_ref[...] = _post_norm(x_ref[...], f, gt_ref[0], g_ref[...], b_ref[...], alpha)


def _mla_out(o, w, x, gate, ln_g, ln_b, alpha, tm):
    n, d = x.shape
    ins = [o, w["w_uv"], w["w_o"], x, gate, ln_g, ln_b]
    return pl.pallas_call(
        functools.partial(_mla_out_kernel, alpha=alpha),
        grid=(n // tm,),
        in_specs=[_tok_spec(tm, o.shape[1]), _full_spec(ins[1]), _full_spec(ins[2]),
                  _tok_spec(tm, d), _mod_spec(gate, n, tm), _full_spec(ln_g), _full_spec(ln_b)],
        out_specs=_tok_spec(tm, d),
        out_shape=jax.ShapeDtypeStruct((n, d), F32),
        scratch_shapes=[pltpu.VMEM((tm, MLA_HEADS * MLA_VDIM), BF16)],
        compiler_params=_cp(("parallel",), BIG_VMEM_LIMIT),
        name="mla_out",
    )(*ins)


RET_COLS = 1024


def _ret_proj_kernel(x_ref, sh_ref, sc_ref, w_ref, c_ref, sa_ref, sb_ref, z_ref):
    j = pl.program_id(1)
    h = (x_ref[...] * (1.0 + sc_ref[0]) + sh_ref[0]).astype(BF16)
    z = jnp.dot(h, w_ref[...], preferred_element_type=F32)

    @pl.when(j < 2)
    def _():
        kscale = jnp.where(j == 1, RET_DK ** -0.5, 1.0).astype(F32)
        z_ref[...] = _rope(z, c_ref[...], sa_ref[...], sb_ref[...], RET_DK // 2) * kscale

    @pl.when(j >= 2)
    def _():
        z_ref[...] = z


def _ret_proj(x, sh, sc, w_in, tabs, tm):
    n, d = x.shape
    f = w_in.shape[1]
    assert RET_HEADS * RET_DK == RET_COLS
    return pl.pallas_call(
        _ret_proj_kernel,
        grid=(n // tm, f // RET_COLS),
        in_specs=[_tok_spec(tm, d), _mod_spec(sh, n, tm), _mod_spec(sc, n, tm),
                  pl.BlockSpec((d, RET_COLS), lambda i, j: (0, j))]
                 + [_tab_spec(t, tm) for t in tabs],
        out_specs=pl.BlockSpec((tm, RET_COLS), lambda i, j: (i, j)),
        out_shape=jax.ShapeDtypeStruct((n, f), F32),
        compiler_params=_cp(("parallel", "arbitrary"), BIG_VMEM_LIMIT),
        name="ret_proj",
    )(x, sh, sc, w_in, *tabs)


def _ret_decay_tables():
    ln = RET_CHUNK
    lg = jnp.log1p(-jnp.exp2(-5.0 - jnp.arange(RET_HEADS, dtype=F32)))
    idx = jnp.arange(ln, dtype=F32)
    diff = idx[:, None] - idx[None, :]
    decay = jnp.where(diff >= 0, jnp.exp(jnp.maximum(diff, 0.0)[None] * lg[:, None, None]), 0.0)
    rs = jnp.exp((idx[None, :] + 1.0) * lg[:, None])[:, :, None]
    wk = jnp.exp((ln - 1.0 - idx)[None, :] * lg[:, None])[:, :, None]
    gl = jnp.exp(ln * lg)[:, None, None]
    g1 = jnp.exp(lg)[:, None, None]
    return decay, rs, wk, gl, g1


def _ret_chunk_kernel(q_ref, k_ref, v_ref, dec_ref, rs_ref, wk_ref, gl_ref, o_ref, sf_ref, s_sc):
    c = pl.program_id(2)

    @pl.when(c == 0)
    def _():
        s_sc[...] = jnp.zeros_like(s_sc)

    q = q_ref[...].astype(BF16)
    k = k_ref[...]
    v = v_ref[...].astype(BF16)
    s_old = s_sc[...]
    inner = lax.dot_general(q, k.astype(BF16), _NT, preferred_element_type=F32) * dec_ref[0]
    o = jnp.dot(inner.astype(BF16), v, preferred_element_type=F32)
    o = o + jnp.dot(q, s_old.astype(BF16), preferred_element_type=F32) * rs_ref[0]
    o_ref[...] = o
    kw_t = (k * wk_ref[0]).T.astype(BF16)
    s_new = gl_ref[0] * s_old + jnp.dot(kw_t, v, preferred_element_type=F32)
    s_sc[...] = s_new

    @pl.when(c == pl.num_programs(2) - 1)
    def _():
        sf_ref[0, 0] = s_new


def _ret_chunks(z, b, s, tables):
    decay, rs, wk, gl, _ = tables
    ln = RET_CHUNK
    nc = s // ln
    hk = RET_HEADS
    kv0 = 2 * hk * RET_DK // RET_DV
    return pl.pallas_call(
        _ret_chunk_kernel,
        grid=(b, hk, nc),
        in_specs=[pl.BlockSpec((ln, RET_DK), lambda bb, hh, c: (bb * nc + c, hh)),
                  pl.BlockSpec((ln, RET_DK), lambda bb, hh, c: (bb * nc + c, hk + hh)),
                  pl.BlockSpec((ln, RET_DV), lambda bb, hh, c: (bb * nc + c, kv0 + hh)),
                  pl.BlockSpec((1, ln, ln), lambda bb, hh, c: (hh, 0, 0)),
                  pl.BlockSpec((1, ln, 1), lambda bb, hh, c: (hh, 0, 0)),
                  pl.BlockSpec((1, ln, 1), lambda bb, hh, c: (hh, 0, 0)),
                  pl.BlockSpec((1, 1, 1), lambda bb, hh, c: (hh, 0, 0))],
        out_specs=[pl.BlockSpec((ln, RET_DV), lambda bb, hh, c: (bb * nc + c, hh)),
                   pl.BlockSpec((1, 1, RET_DK, RET_DV), lambda bb, hh, c: (bb, hh, 0, 0))],
        out_shape=[jax.ShapeDtypeStruct((b * s, hk * RET_DV), F32),
                   jax.ShapeDtypeStruct((b, hk, RET_DK, RET_DV), F32)],
        scratch_shapes=[pltpu.VMEM((RET_DK, RET_DV), F32)],
        compiler_params=_cp(("parallel", "parallel", "arbitrary")),
        name="ret_chunks",
    )(z, z, z, decay, rs, wk, gl)


def _ret_step_kernel(s_ref, q_ref, k_ref, v_ref, g1_ref, o_ref, sn_ref):
    s_old = s_ref[0, 0]
    qc = q_ref[0, 0]
    kc = k_ref[0, 0]
    v = v_ref[0, 0]
    gamma = g1_ref[0]
    inner = jnp.sum(qc * kc, axis=0, keepdims=True)
    o_ref[0, 0] = inner * v + jnp.sum(qc * s_old, axis=0, keepdims=True) * gamma
    sn_ref[0, 0] = gamma * s_old + kc * v


def _ret_step(state, z, g1):
    db = state.shape[0]
    hk = RET_HEADS
    q = z[:, :hk * RET_DK].reshape(db, hk, RET_DK, 1)
    k = z[:, hk * RET_DK:2 * hk * RET_DK].reshape(db, hk, RET_DK, 1)
    v = z[:, 2 * hk * RET_DK:2 * hk * RET_DK + hk * RET_DV].reshape(db, hk, 1, RET_DV)
    col = pl.BlockSpec((1, 1, RET_DK, 1), lambda bb, hh: (bb, hh, 0, 0))
    row = pl.BlockSpec((1, 1, 1, RET_DV), lambda bb, hh: (bb, hh, 0, 0))
    st = pl.BlockSpec((1, 1, RET_DK, RET_DV), lambda bb, hh: (bb, hh, 0, 0))
    o, s_new = pl.pallas_call(
        _ret_step_kernel,
        grid=(db, hk),
        in_specs=[st, col, col, row, pl.BlockSpec((1, 1, 1), lambda bb, hh: (hh, 0, 0))],
        out_specs=[row, st],
        out_shape=[jax.ShapeDtypeStruct((db, hk, 1, RET_DV), F32),
                   jax.ShapeDtypeStruct(state.shape, F32)],
        compiler_params=_cp(("parallel", "parallel")),
        name="ret_step",
    )(state, q, k, v, g1)
    return o.reshape(db, hk * RET_DV), s_new


def _ret_out_kernel(o_ref, gz_ref, gn_ref, wo_ref, x_ref, gt_ref, g_ref, b_ref, y_ref, t_sc, *, alpha):
    for hh in range(RET_HEADS):
        oh = o_ref[:, hh * RET_DV:(hh + 1) * RET_DV]
        mu = jnp.mean(oh, axis=-1, keepdims=True)
        oc = oh - mu
        var = jnp.mean(oc * oc, axis=-1, keepdims=True)
        yh = oc * lax.rsqrt(var + LN_EPS) * gn_ref[:, hh * RET_DV:(hh + 1) * RET_DV]
        gz = gz_ref[:, hh * RET_DV:(hh + 1) * RET_DV]
        t_sc[:, hh * RET_DV:(hh + 1) * RET_DV] = (gz * jax.nn.sigmoid(gz) * yh).astype(BF16)
    f = jnp.dot(t_sc[...], wo_ref[...], preferred_element_type=F32)
    y_ref[...] = _post_norm(x_ref[...], f, gt_ref[0], g_ref[...], b_ref[...], alpha)


def _ret_out(o, z, gn, w_o, x, gate, ln_g, ln_b, alpha, tm):
    n, d = x.shape
    hv = RET_HEADS * RET_DV
    gblk = z.shape[1] // hv - 1
    return pl.pallas_call(
        functools.partial(_ret_out_kernel, alpha=alpha),
        grid=(n // tm,),
        in_specs=[_tok_spec(tm, hv), pl.BlockSpec((tm, hv), lambda i: (i, gblk)),
                  _full_spec(gn), _full_spec(w_o), _tok_spec(tm, d), _mod_spec(gate, n, tm),
                  _full_spec(ln_g), _full_spec(ln_b)],
        out_specs=_tok_spec(tm, d),
        out_shape=jax.ShapeDtypeStruct((n, d), F32),
        scratch_shapes=[pltpu.VMEM((tm, hv), BF16)],
        compiler_params=_cp(("parallel",), BIG_VMEM_LIMIT),
        name="ret_out",
    )(o, z, gn, w_o, x, gate, ln_g, ln_b)


def _moba_proj_kernel(x_ref, sh_ref, sc_ref, w_ref, c_ref, sa_ref, sb_ref,
                      q_ref, k_ref, v_ref, kb_ref, vt_ref, km_ref):
    hd, dh = MOBA_HEADS, MOBA_HEAD_DIM
    h = (x_ref[...] * (1.0 + sc_ref[0]) + sh_ref[0]).astype(BF16)
    z = jnp.dot(h, w_ref[...], preferred_element_type=F32)
    c, sa, sb = c_ref[...], sa_ref[...], sb_ref[...]
    q = _rope(z[:, :hd * dh], c, sa, sb, MOBA_ROT_DIM // 2)
    k = _rope(z[:, hd * dh:2 * hd * dh], c, sa, sb, MOBA_ROT_DIM // 2)
    q_ref[...] = q.astype(BF16)
    for hh in range(hd):
        kh = k[:, hh * dh:(hh + 1) * dh]
        vh = z[:, (2 * hd + hh) * dh:(2 * hd + hh + 1) * dh]
        k_ref[0, hh] = kh
        v_ref[0, hh] = vh
        kb_ref[0, hh] = kh.astype(BF16)
        vt_ref[0, hh] = vh.T.astype(BF16)
        km_ref[0, 0, hh:hh + 1, :] = jnp.mean(kh, axis=0, keepdims=True)


def _moba_proj(x, sh, sc, w_qkv, tabs, b, s, tm):
    n, d = x.shape
    hd, dh = MOBA_HEADS, MOBA_HEAD_DIM
    nt = s // tm
    kv_spec = pl.BlockSpec((1, hd, tm, dh), lambda i: (i // nt, 0, i % nt, 0))
    kv_shape = (b, hd, s, dh)
    return pl.pallas_call(
        _moba_proj_kernel,
        grid=(n // tm,),
        in_specs=[_tok_spec(tm, d), _mod_spec(sh, n, tm), _mod_spec(sc, n, tm), _full_spec(w_qkv)]
                 + [_tab_spec(t, tm) for t in tabs],
        out_specs=[_tok_spec(tm, hd * dh), kv_spec, kv_spec, kv_spec,
                   pl.BlockSpec((1, hd, dh, tm), lambda i: (i // nt, 0, 0, i % nt)),
                   pl.BlockSpec((1, 1, hd, dh), lambda i: (i // nt, i % nt, 0, 0))],
        out_shape=[jax.ShapeDtypeStruct((n, hd * dh), BF16),
                   jax.ShapeDtypeStruct(kv_shape, F32), jax.ShapeDtypeStruct(kv_shape, F32),
                   jax.ShapeDtypeStruct(kv_shape, BF16), jax.ShapeDtypeStruct((b, hd, dh, s), BF16),
                   jax.ShapeDtypeStruct((b, nt, hd, dh), F32)],
        compiler_params=_cp(("parallel",), BIG_VMEM_LIMIT),
        name="moba_proj",
    )(x, sh, sc, w_qkv, *tabs)


def _moba_attn_kernel(q_ref, k_ref, vt_ref, km_ref, o_ref, m_sc, l_sc, acc_sc, sel_sc, *, nb):
    hd, dh, blk = MOBA_HEADS, MOBA_HEAD_DIM, MOBA_BLOCK
    i, jj = _fold_causal(pl.program_id(1), pl.program_id(2), nb, lambda q: q + 1)

    def scores(hh):
        q = q_ref[:, hh * dh:(hh + 1) * dh]
        return lax.dot_general(k_ref[0, hh], q, _NT, preferred_element_type=F32) * MOBA_SCALE

    @pl.when(jj == 0)
    def _():
        rowb = lax.broadcasted_iota(jnp.int32, (LANES, blk), 0)
        rowbf = rowb.astype(F32)
        kpos = lax.broadcasted_iota(jnp.int32, (blk, blk), 0)
        qpos = lax.broadcasted_iota(jnp.int32, (blk, blk), 1)
        for hh in range(hd):
            km = km_ref[0, hh].astype(BF16)
            gate = lax.dot_general(km, q_ref[:, hh * dh:(hh + 1) * dh], _NT, preferred_element_type=F32)
            gate = jnp.where(rowb < i, gate, -jnp.inf)
            sel = jnp.zeros((LANES, blk), F32)
            for _ in range(min(MOBA_TOPK, nb)):
                mx = jnp.max(gate, axis=0, keepdims=True)
                idx = jnp.min(jnp.where(gate == mx, rowbf, float(LANES)), axis=0, keepdims=True)
                hit = rowbf == idx
                sel = jnp.where(hit & (rowb < i), 1.0, sel)
                gate = jnp.where(hit, -jnp.inf, gate)
            for n in range(nb):
                sel_sc[hh, n] = sel[n:n + 1, :]
            st = jnp.where(kpos <= qpos, scores(hh), NEG)
            mx = jnp.max(st, axis=0, keepdims=True)
            p = jnp.exp(st - mx)
            m_sc[hh] = mx
            l_sc[hh] = jnp.sum(p, axis=0, keepdims=True)
            acc_sc[hh] = jnp.dot(vt_ref[0, hh], p.astype(BF16), preferred_element_type=F32)

    @pl.when(jj > 0)
    def _():
        for hh in range(hd):
            picked = sel_sc[hh, jj - 1]
            st = jnp.where(picked > 0.5, scores(hh), NEG)
            m_prev = m_sc[hh]
            m_new = jnp.maximum(m_prev, jnp.max(st, axis=0, keepdims=True))
            alpha = jnp.exp(m_prev - m_new)
            p = jnp.exp(st - m_new)
            l_sc[hh] = alpha * l_sc[hh] + jnp.sum(p, axis=0, keepdims=True)
            acc_sc[hh] = alpha * acc_sc[hh] + jnp.dot(vt_ref[0, hh], p.astype(BF16),
                                                      preferred_element_type=F32)
            m_sc[hh] = m_new

    @pl.when(jj == i)
    def _():
        for hh in range(hd):
            o_ref[:, hh * dh:(hh + 1) * dh] = (acc_sc[hh] / l_sc[hh]).T.astype(o_ref.dtype)


def _moba_attn(q, kb, vt, kmean, b, s):
    hd, dh, blk = MOBA_HEADS, MOBA_HEAD_DIM, MOBA_BLOCK
    nb = s // blk
    assert nb <= LANES and nb % 2 == 0

    def count(i):
        return i + 1

    def own_and_step(p, t):
        return _fold_causal(p, t, nb, count)

    def kv_block(p, t):
        i, jj = own_and_step(p, t)
        return jnp.where(jj == 0, i, jj - 1)

    qmap = lambda bb, p, t: (bb * nb + own_and_step(p, t)[0], 0)
    return pl.pallas_call(
        functools.partial(_moba_attn_kernel, nb=nb),
        grid=(b, nb // 2, nb + 1),
        in_specs=[pl.BlockSpec((blk, hd * dh), qmap),
                  pl.BlockSpec((1, hd, blk, dh), lambda bb, p, t: (bb, 0, kv_block(p, t), 0)),
                  pl.BlockSpec((1, hd, dh, blk), lambda bb, p, t: (bb, 0, 0, kv_block(p, t))),
                  pl.BlockSpec((1, hd, LANES, dh), lambda bb, p, t: (bb, 0, 0, 0))],
        out_specs=pl.BlockSpec((blk, hd * dh), qmap),
        out_shape=jax.ShapeDtypeStruct((b * s, hd * dh), BF16),
        scratch_shapes=[pltpu.VMEM((hd, 1, blk), F32), pltpu.VMEM((hd, 1, blk), F32),
                        pltpu.VMEM((hd, dh, blk), F32), pltpu.VMEM((hd, nb, 1, blk), F32)],
        compiler_params=_cp(("parallel", "parallel", "arbitrary"), BIG_VMEM_LIMIT),
        name="moba_attn",
    )(q, kb, vt, kmean)


MOBA_DEC_PAGES = 8


def _moba_dec_mean_kernel(pt_ref, q_ref, *rest, ppb, own):
    gp = MOBA_DEC_PAGES
    pages = rest[:gp]
    sel_ref = rest[gp]
    km_sc = rest[gp + 1]
    g = pl.program_id(1)
    inv = 1.0 / (ppb * PAGE_SIZE)
    for t in range(0, gp, ppb):
        acc = jnp.sum(pages[t][0, 0], axis=1)
        for r in range(1, ppb):
            acc = acc + jnp.sum(pages[t + r][0, 0], axis=1)
        km_sc[g * (gp // ppb) + t // ppb] = acc * inv

    @pl.when(g == pl.num_programs(1) - 1)
    def _():
        gate = jnp.sum(km_sc[...] * q_ref[0][None], axis=-1)
        rowf = lax.broadcasted_iota(jnp.int32, gate.shape, 0).astype(F32)
        for t in range(min(MOBA_TOPK, own)):
            mx = jnp.max(gate, axis=0, keepdims=True)
            idx = jnp.min(jnp.where(gate == mx, rowf, float(own)), axis=0, keepdims=True)
            sel_ref[0, t:t + 1, :] = idx.astype(jnp.int32)
            gate = jnp.where(rowf == idx, -jnp.inf, gate)


def _moba_dec_select(page_table, q, pool_k, layer):
    db, n_pages = page_table.shape
    hd, dh = MOBA_HEADS, MOBA_HEAD_DIM
    ppb = MOBA_BLOCK // PAGE_SIZE
    own = n_pages // ppb
    gp = MOBA_DEC_PAGES
    assert gp % ppb == 0 and (own * ppb) % gp == 0
    topk = min(MOBA_TOPK, own)

    def page_spec(t):
        return pl.BlockSpec((1, 1, hd, PAGE_SIZE, dh),
                            lambda bb, g, pt: (layer, pt[bb, g * gp + t], 0, 0, 0))

    return pl.pallas_call(
        functools.partial(_moba_dec_mean_kernel, ppb=ppb, own=own),
        grid_spec=pltpu.PrefetchScalarGridSpec(
            num_scalar_prefetch=1,
            grid=(db, own * ppb // gp),
            in_specs=[pl.BlockSpec((1, hd, dh), lambda bb, g, pt: (bb, 0, 0))]
                     + [page_spec(t) for t in range(gp)],
            out_specs=pl.BlockSpec((1, topk, hd), lambda bb, g, pt: (bb, 0, 0)),
            scratch_shapes=[pltpu.VMEM((own, hd, dh), F32)]),
        out_shape=jax.ShapeDtypeStruct((db, topk, hd), jnp.int32),
        compiler_params=_cp(("parallel", "arbitrary")),
        name="moba_dec_select",
    )(page_table, q, *([pool_k] * gp))


def _moba_dec_attn_kernel(pt_ref, sel_ref, q_ref, kn_ref, vn_ref, *rest, npg):
    kp = rest[:npg]
    vp = rest[npg:2 * npg]
    o_ref = rest[2 * npg]
    q = q_ref[0, 0]
    s_self = jnp.sum(q * kn_ref[0, 0], axis=1, keepdims=True) * MOBA_SCALE
    ss = [jnp.sum(kp[t][0, 0, 0] * q, axis=1, keepdims=True) * MOBA_SCALE for t in range(npg)]
    mx = s_self
    for s in ss:
        mx = jnp.maximum(mx, jnp.max(s, axis=0, keepdims=True))
    p_self = jnp.exp(s_self - mx)
    den = p_self
    acc = p_self * vn_ref[0, 0]
    for t in range(npg):
        p = jnp.exp(ss[t] - mx)
        den = den + jnp.sum(p, axis=0, keepdims=True)
        acc = acc + jnp.sum(p * vp[t][0, 0, 0], axis=0, keepdims=True)
    o_ref[0, 0] = acc / den


def _moba_dec_attn(page_table, sel, q, kn, vn, pool_k, pool_v, layer):
    db = page_table.shape[0]
    hd, dh = MOBA_HEADS, MOBA_HEAD_DIM
    ppb = MOBA_BLOCK // PAGE_SIZE
    topk = sel.shape[1]
    npg = topk * ppb

    def page_spec(t):
        kk, r = t // ppb, t % ppb
        return pl.BlockSpec((1, 1, 1, PAGE_SIZE, dh),
                            lambda bb, hh, pt, sl: (layer, pt[bb, sl[bb, kk, hh] * ppb + r], hh, 0, 0))

    vec = pl.BlockSpec((1, 1, 1, dh), lambda bb, hh, pt, sl: (bb, hh, 0, 0))
    return pl.pallas_call(
        functools.partial(_moba_dec_attn_kernel, npg=npg),
        grid_spec=pltpu.PrefetchScalarGridSpec(
            num_scalar_prefetch=2,
            grid=(db, hd),
            in_specs=[vec, vec, vec] + [page_spec(t) for t in range(npg)] * 2,
            out_specs=vec),
        out_shape=jax.ShapeDtypeStruct((db, hd, 1, dh), F32),
        compiler_params=_cp(("parallel", "parallel")),
        name="moba_dec_attn",
    )(page_table, sel, q, kn, vn, *([pool_k] * npg), *([pool_v] * npg))


def _proj_out_kernel(o_ref, wo_ref, x_ref, gt_ref, g_ref, b_ref, y_ref, *, alpha):
    f = jnp.dot(o_ref[...].astype(BF16), wo_ref[...], preferred_element_type=F32)
    y_ref[...] = _post_norm(x_ref[...], f, gt_ref[0], g_ref[...], b_ref[...], alpha)


def _proj_out(o, w_o, x, gate, ln_g, ln_b, alpha, tm):
    n, d = x.shape
    return pl.pallas_call(
        functools.partial(_proj_out_kernel, alpha=alpha),
        grid=(n // tm,),
        in_specs=[_tok_spec(tm, o.shape[1]), _full_spec(w_o), _tok_spec(tm, d),
                  _mod_spec(gate, n, tm), _full_spec(ln_g), _full_spec(ln_b)],
        out_specs=_tok_spec(tm, d),
        out_shape=jax.ShapeDtypeStruct((n, d), F32),
        compiler_params=_cp(("parallel",), BIG_VMEM_LIMIT),
        name="proj_out",
    )(o, w_o, x, gate, ln_g, ln_b)


PEER_DENSE_TOKENS = 512
_PEER_PAIRS = [(i, j) for i in range(PEER_TOPK) for j in range(PEER_TOPK) if (i + 1) * (j + 1) <= PEER_TOPK]


def _peer_topk_kernel(x_ref, sh_ref, sc_ref, wq_ref, k1_ref, k2_ref,
                      h_ref, ca_ref, e1_ref, r2_ref, e2_ref,
                      qb_sc, s1_sc, s2_sc, v1_sc, v2_sc, i1_sc, *, tm):
    nk, hd, kt = PEER_N_KEYS, PEER_HEADS, PEER_TOPK
    half = PEER_KEY_DIM // 2
    h = (x_ref[...] * (1.0 + sc_ref[0]) + sh_ref[0]).astype(BF16)
    h_ref[...] = h
    qb_sc[...] = jnp.dot(h, wq_ref[...], preferred_element_type=F32).astype(BF16)
    iota_k = lax.broadcasted_iota(jnp.int32, (nk, LANES), 0).astype(F32)
    chunks = [slice(c * LANES, (c + 1) * LANES) for c in range(tm // LANES)]

    def pick(s):
        m = jnp.max(s, axis=0, keepdims=True)
        idx = jnp.min(jnp.where(s == m, iota_k, float(nk)), axis=0, keepdims=True)
        return m, idx, iota_k == idx

    for hh in range(hd):
        s1_sc[...] = lax.dot_general(k1_ref[...], qb_sc[:, hh * 2 * half:hh * 2 * half + half], _NT,
                                     preferred_element_type=F32)
        s2_sc[...] = lax.dot_general(k2_ref[...], qb_sc[:, hh * 2 * half + half:(hh + 1) * 2 * half], _NT,
                                     preferred_element_type=F32)
        for lanes in chunks:
            def body(k, carry, lanes=lanes, hh=hh):
                s1, s2, r2 = carry
                m1, idx1, hit1 = pick(s1)
                m2, _, hit2 = pick(s2)
                v1_sc[k, hh:hh + 1, lanes] = m1
                i1_sc[k, hh:hh + 1, lanes] = idx1
                v2_sc[k, hh:hh + 1, lanes] = m2
                return (jnp.where(hit1, -jnp.inf, s1), jnp.where(hit2, -jnp.inf, s2),
                        jnp.where(hit2, k.astype(F32), r2))

            s2_0 = s2_sc[:, lanes]
            _, _, r2 = lax.fori_loop(0, kt, body, (s1_sc[:, lanes], s2_0, jnp.full((nk, LANES), float(kt), F32)))
            r2_ref[hh, :, lanes] = r2
            e2_ref[hh, :, lanes] = jnp.exp(s2_0 - v2_sc[0, hh:hh + 1, lanes])

    pos = [float(i * kt + j) for i, j in _PEER_PAIRS]
    for lanes in chunks:
        v1 = [v1_sc[i, :, lanes] for i in range(kt)]
        v2 = [v2_sc[j, :, lanes] for j in range(kt)]
        cand0 = tuple(v1[i] + v2[j] for i, j in _PEER_PAIRS)
        m0 = cand0[0]
        zero = jnp.zeros((hd, LANES), F32)

        def cbody(_, carry, m0=m0):
            cand, cnt, z = carry
            m = cand[0]
            for cv in cand[1:]:
                m = jnp.maximum(m, cv)
            pmin = jnp.full((hd, LANES), 1e9, F32)
            for cv, pp in zip(cand, pos):
                pmin = jnp.minimum(pmin, jnp.where(cv == m, pp, 1e9))
            new_cand = []
            new_cnt = list(cnt)
            for (ci, _cj), cv, pp in zip(_PEER_PAIRS, cand, pos):
                hit = pmin == pp
                new_cand.append(jnp.where(hit, -jnp.inf, cv))
                new_cnt[ci] = new_cnt[ci] + jnp.where(hit, 1.0, 0.0)
            return tuple(new_cand), tuple(new_cnt), z + jnp.exp(m - m0)

        _, cnt, z = lax.fori_loop(0, kt, cbody, (cand0, (zero,) * kt, zero))
        zi = 1.0 / z
        e1k = [jnp.exp(v1[k] - v1[0]) * zi for k in range(kt)]
        i1 = [i1_sc[k, :, lanes] for k in range(kt)]

        def abody(a, _, cnt=cnt, e1k=e1k, i1=i1, lanes=lanes):
            af = a.astype(F32)
            ca = jnp.zeros((hd, LANES), F32)
            e1 = jnp.zeros((hd, LANES), F32)
            for k in range(kt):
                hit = i1[k] == af
                ca = jnp.where(hit, cnt[k], ca)
                e1 = jnp.where(hit, e1k[k], e1)
            ca_ref[a, :, lanes] = ca
            e1_ref[a, :, lanes] = e1
            return 0

        lax.fori_loop(0, nk, abody, 0)


def _peer_topk(x, sh, sc, w_q, k1, k2, tm):
    n, d = x.shape
    nk, hd, kt = PEER_N_KEYS, PEER_HEADS, PEER_TOPK
    row_spec = pl.BlockSpec((nk, hd, tm), lambda i: (0, 0, i))
    sel_spec = pl.BlockSpec((hd, nk, tm), lambda i: (0, 0, i))
    row_shape = jax.ShapeDtypeStruct((nk, hd, n), F32)
    sel_shape = jax.ShapeDtypeStruct((hd, nk, n), F32)
    return pl.pallas_call(
        functools.partial(_peer_topk_kernel, tm=tm),
        grid=(n // tm,),
        in_specs=[_tok_spec(tm, d), _mod_spec(sh, n, tm), _mod_spec(sc, n, tm),
                  _full_spec(w_q), _full_spec(k1), _full_spec(k2)],
        out_specs=[_tok_spec(tm, d), row_spec, row_spec, sel_spec, sel_spec],
        out_shape=[jax.ShapeDtypeStruct((n, d), BF16), row_shape, row_shape, sel_shape, sel_shape],
        scratch_shapes=[pltpu.VMEM((tm, hd * PEER_KEY_DIM), BF16),
                        pltpu.VMEM((nk, tm), F32), pltpu.VMEM((nk, tm), F32),
                        pltpu.VMEM((kt, hd, tm), F32), pltpu.VMEM((kt, hd, tm), F32),
                        pltpu.VMEM((kt, hd, tm), F32)],
        compiler_params=_cp(("parallel",), BIG_VMEM_LIMIT),
        name="peer_topk",
    )(x, sh, sc, w_q, k1, k2)


def _peer_dense_kernel(h_ref, u_ref, vt_ref, ca_ref, e1_ref, r2_ref, e2_ref,
                       x_ref, gt_ref, g_ref, b_ref, y_ref, a_sc, p_sc, acc_sc, r2_sc, e2_sc,
                       *, na, tm, alpha):
    nk, hd = PEER_N_KEYS, PEER_HEADS
    j = pl.program_id(1)

    @pl.when(j == 0)
    def _():
        acc_sc[...] = jnp.zeros_like(acc_sc)
        for hh in range(hd):
            r2_sc[hh] = r2_ref[hh].astype(BF16)
            e2_sc[hh] = e2_ref[hh].astype(BF16)

    hu = lax.dot_general(u_ref[...], h_ref[...], _NT, preferred_element_type=F32)
    a_sc[...] = (0.5 * hu * (1.0 + lax.erf(hu * INV_SQRT2))).astype(BF16)
    zero = jnp.zeros((nk, LANES), BF16)
    for ai in range(na):
        rows = slice(ai * nk, (ai + 1) * nk)
        for c in range(tm // LANES):
            lanes = slice(c * LANES, (c + 1) * LANES)
            w = zero
            for hh in range(hd):
                ca = jnp.broadcast_to(ca_ref[ai, hh:hh + 1, lanes], (nk, LANES)).astype(BF16)
                e1 = jnp.broadcast_to(e1_ref[ai, hh:hh + 1, lanes], (nk, LANES)).astype(BF16)
                w = w + jnp.where(r2_sc[hh, :, lanes] < ca, e2_sc[hh, :, lanes] * e1, zero)
            p_sc[rows, lanes] = w * a_sc[rows, lanes]
    acc_sc[...] += jnp.dot(vt_ref[...], p_sc[...], preferred_element_type=F32)

    @pl.when(j == pl.num_programs(1) - 1)
    def _():
        f = acc_sc[...].T
        y_ref[...] = _post_norm(x_ref[...], f, gt_ref[0], g_ref[...], b_ref[...], alpha)


def _peer_dense(h, u, vt, sel, x, gate, ln_g, ln_b, alpha, tm, na):
    n, d = x.shape
    nk, hd = PEER_N_KEYS, PEER_HEADS
    te = na * nk
    ne = u.shape[0]
    sel_spec = pl.BlockSpec((hd, nk, tm), lambda i, j: (0, 0, i))
    row_spec = pl.BlockSpec((na, hd, tm), lambda i, j: (j, 0, i))
    ca, e1, r2, e2 = sel
    return pl.pallas_call(
        functools.partial(_peer_dense_kernel, na=na, tm=tm, alpha=alpha),
        grid=(n // tm, ne // te),
        in_specs=[_tok_spec(tm, d), pl.BlockSpec((te, d), lambda i, j: (j, 0)),
                  pl.BlockSpec((d, te), lambda i, j: (0, j)),
                  row_spec, row_spec, sel_spec, sel_spec,
                  _tok_spec(tm, d), _mod_spec(gate, n, tm), _full_spec(ln_g), _full_spec(ln_b)],
        out_specs=_tok_spec(tm, d),
        out_shape=jax.ShapeDtypeStruct((n, d), F32),
        scratch_shapes=[pltpu.VMEM((te, tm), BF16), pltpu.VMEM((te, tm), BF16), pltpu.VMEM((d, tm), F32),
                        pltpu.VMEM((hd, nk, tm), BF16), pltpu.VMEM((hd, nk, tm), BF16)],
        compiler_params=_cp(("parallel", "arbitrary"), BIG_VMEM_LIMIT),
        name="peer_dense",
    )(h, u, vt, ca, e1, r2, e2, x, gate, ln_g, ln_b)


def _mods(m, ls, b, db, d):
    mp = m[ls, :b]
    ms = m[ls, b:b + db]
    prompt = tuple(mp[:, None, k * d:(k + 1) * d] for k in range(3))
    sample = tuple(ms[None, :, k * d:(k + 1) * d] for k in range(3))
    return prompt, sample


def kernel(x_prompt, x_sample, cache_mla_ckv, cache_mla_kpe, cache_moba_k, cache_moba_v, state_ret, page_table, c_prompt, c_sample, ada_w, ada_b, ln_g, ln_b, mla_w_in, mla_q_norm, mla_kv_norm, mla_w_uq, mla_w_uk, mla_w_uv, mla_w_o, ret_w_in, ret_gn, ret_w_o, moba_w_qkv, moba_w_o, peer_w_q, peer_k1, peer_k2, peer_u, peer_v):
    b, s, d = x_prompt.shape
    db, t_new, _ = x_sample.shape
    assert t_new == 1
    depth = ada_w.shape[0]
    alpha = (2 * depth) ** 0.25
    n_pages = page_table.shape[1]
    past = n_pages * PAGE_SIZE
    n_p = b * s
    tm_p = min(256, s)
    tm_s = db
    assert s % MOBA_BLOCK == 0 and tm_p == MOBA_BLOCK and db % 16 == 0

    pad = (-(b + db)) % 8
    c_all = jnp.concatenate([c_prompt, c_sample, jnp.zeros((pad, d), F32)], axis=0)
    mod = _adaln(c_all, ada_w, ada_b)

    pos_p = jnp.arange(s)
    pos_s = jnp.full((1,), past)
    cache_kpe_t = jnp.swapaxes(cache_mla_kpe, 2, 3)
    ret_tabs = _ret_decay_tables()

    xp = x_prompt.reshape(n_p, d)
    xs = x_sample.reshape(db, d)
    outs = {k: [] for k in ("ckv_p", "kpe_p", "ckv_s", "kpe_s", "mk_p", "mv_p", "mk_s", "mv_s", "rs_p", "rs_s")}

    for i in range(depth):
        j = i // N_MIXERS
        kind = i % N_MIXERS
        (shp, scp, gtp), (shs, scs, gts) = _mods(mod, 2 * i, b, db, d)
        g0, b0 = ln_g[i, 0][None], ln_b[i, 0][None]
        if kind == 0:
            hd = MLA_HEADS
            w_in = jnp.pad(mla_w_in[j], ((0, 0), (0, LANES - MLA_ROPE))).astype(BF16)
            w_uq = mla_w_uq[j].reshape(MLA_Q_RANK, hd, MLA_NOPE + MLA_ROPE)
            w_uq = jnp.concatenate([w_uq[:, :, :MLA_NOPE].reshape(MLA_Q_RANK, hd * MLA_NOPE),
                                    w_uq[:, :, MLA_NOPE:].reshape(MLA_Q_RANK, hd * MLA_ROPE)], axis=1)
            w = dict(w_in=w_in, q_norm=mla_q_norm[j][None], kv_norm=mla_kv_norm[j][None],
                     w_uq=w_uq.astype(BF16),
                     w_uk=jnp.transpose(mla_w_uk[j], (1, 2, 0)).astype(BF16),
                     w_uv=jnp.transpose(mla_w_uv[j], (1, 0, 2)).astype(BF16),
                     w_o=mla_w_o[j].astype(BF16))
            for x, sh, sc, gt, pos, tm, tag in ((xp, shp, scp, gtp, pos_p, tm_p, "p"), (xs, shs, scs, gts, pos_s, tm_s, "s")):
                tq = _rope_tables(pos, MLA_ROPE_THETA, MLA_ROPE, MLA_ROPE, hd)
                tk = _rope_tables(pos, MLA_ROPE_THETA, MLA_ROPE, LANES, 1)
                if tag == "p":
                    ckv, kpe, kcat, vt, qcat = _mla_proj(x, sh, sc, w, tq, tk, b, s, tm)
                    o = _mla_attn(qcat, kcat, vt, b, s)
                    outs["ckv_p"].append(ckv.reshape(b, s, MLA_KV_RANK))
                    outs["kpe_p"].append(kpe.reshape(b, s, MLA_ROPE))
                else:
                    ckv, kpe, _, _, qcat = _mla_proj(x, sh, sc, w, tq, tk, 1, db, tm)
                    qcat = qcat.reshape(db, hd, MLA_QK_WIDTH)
                    rpad = ((0, 0), (0, MLA_DEC_ROWS - hd), (0, 0))
                    o = _mla_dec(page_table,
                                 jnp.pad(qcat[:, :, :MLA_KV_RANK], rpad),
                                 jnp.pad(qcat[:, :, MLA_KV_RANK:MLA_KV_RANK + MLA_ROPE], rpad),
                                 ckv[:, None, :], kpe[:, None, :], cache_mla_ckv, cache_kpe_t, j)
                    o = o[:, :hd].reshape(db, hd * MLA_KV_RANK)
                    outs["ckv_s"].append(ckv.reshape(db, 1, MLA_KV_RANK))
                    outs["kpe_s"].append(kpe.reshape(db, 1, MLA_ROPE))
                y = _mla_out(o, w, x, gt, g0, b0, alpha, tm)
                if tag == "p":
                    xp = y
                else:
                    xs = y
        elif kind == 1:
            w_in = ret_w_in[j].astype(BF16)
            w_o = ret_w_o[j].astype(BF16)
            gn = ret_gn[j][None]
            tp = _rope_tables(pos_p, RET_ROPE_THETA, RET_DK, RET_DK, RET_HEADS)
            ts = _rope_tables(pos_s, RET_ROPE_THETA, RET_DK, RET_DK, RET_HEADS)
            zp = _ret_proj(xp, shp, scp, w_in, tp, tm_p)
            op, sfin = _ret_chunks(zp, b, s, ret_tabs)
            xp = _ret_out(op, zp, gn, w_o, xp, gtp, g0, b0, alpha, tm_p)
            zs = _ret_proj(xs, shs, scs, w_in, ts, tm_s)
            os_, snew = _ret_step(state_ret[j], zs, ret_tabs[4])
            xs = _ret_out(os_, zs, gn, w_o, xs, gts, g0, b0, alpha, tm_s)
            outs["rs_p"].append(sfin)
            outs["rs_s"].append(snew)
        else:
            hd, dh = MOBA_HEADS, MOBA_HEAD_DIM
            w_qkv = moba_w_qkv[j].astype(BF16)
            w_o = moba_w_o[j].astype(BF16)
            tp = _rope_tables(pos_p, MOBA_ROPE_THETA, MOBA_ROT_DIM, dh, hd)
            ts = _rope_tables(pos_s, MOBA_ROPE_THETA, MOBA_ROT_DIM, dh, hd)
            q, k, v, kb, vtr, kmean = _moba_proj(xp, shp, scp, w_qkv, tp, b, s, tm_p)
            nb = s // MOBA_BLOCK
            kmean = jnp.pad(jnp.transpose(kmean, (0, 2, 1, 3)), ((0, 0), (0, 0), (0, LANES - nb), (0, 0)))
            op = _moba_attn(q, kb, vtr, kmean, b, s)
            xp = _proj_out(op, w_o, xp, gtp, g0, b0, alpha, tm_p)
            outs["mk_p"].append(k)
            outs["mv_p"].append(v)
            qs, ks, vs, _, _, _ = _moba_proj(xs, shs, scs, w_qkv, ts, 1, db, tm_s)
            qs = qs.astype(F32).reshape(db, hd, dh)
            ks = jnp.transpose(ks[0], (1, 0, 2))[:, :, None, :]
            vs = jnp.transpose(vs[0], (1, 0, 2))[:, :, None, :]
            sel = _moba_dec_select(page_table, qs, cache_moba_k, j)
            os_ = _moba_dec_attn(page_table, sel, qs[:, :, None, :], ks, vs, cache_moba_k, cache_moba_v, j)
            xs = _proj_out(os_.reshape(db, hd * dh), w_o, xs, gts, g0, b0, alpha, tm_s)
            outs["mk_s"].append(ks)
            outs["mv_s"].append(vs)

        (shp, scp, gtp), (shs, scs, gts) = _mods(mod, 2 * i + 1, b, db, d)
        g1, b1 = ln_g[i, 1][None], ln_b[i, 1][None]
        w_q = peer_w_q[i].astype(BF16)
        k1 = peer_k1[i].astype(BF16)
        k2 = peer_k2[i].astype(BF16)
        u = peer_u[i].astype(BF16)
        vt = peer_v[i].T.astype(BF16)
        hp, *selp = _peer_topk(xp, shp, scp, w_q, k1, k2, tm_p)
        xp = _peer_dense(hp, u, vt, selp, xp, gtp, g1, b1, alpha, min(PEER_DENSE_TOKENS, s), 4)
        hs, *sels = _peer_topk(xs, shs, scs, w_q, k1, k2, tm_s)
        xs = _peer_dense(hs, u, vt, sels, xs, gts, g1, b1, alpha, tm_s, 4)

    return (xp.reshape(b, s, d), xs.reshape(db, 1, d),
            jnp.stack(outs["ckv_p"]), jnp.stack(outs["kpe_p"]), jnp.stack(outs["ckv_s"]), jnp.stack(outs["kpe_s"]),
            jnp.stack(outs["mk_p"]), jnp.stack(outs["mv_p"]), jnp.stack(outs["mk_s"]), jnp.stack(outs["mv_s"]),
            jnp.stack(outs["rs_p"]), jnp.stack(outs["rs_s"]))
```

```python
import functools
import math

import jax
import jax.numpy as jnp
from jax import lax
from jax.experimental import pallas as pl
from jax.experimental.pallas import tpu as pltpu

F32 = jnp.float32
BF16 = jnp.bfloat16

N_MIXERS = 3
LN_EPS = 1e-5
RMS_EPS = 1e-6
PAGE_SIZE = 128

MLA_HEADS = 8
MLA_NOPE = 128
MLA_ROPE = 64
MLA_VDIM = 128
MLA_KV_RANK = 256
MLA_Q_RANK = 512
MLA_ROPE_THETA = 10000.0
MLA_SCALE = (MLA_NOPE + MLA_ROPE) ** -0.5
MLA_QK_WIDTH = MLA_KV_RANK + 128

RET_HEADS = 4
RET_DK = 256
RET_DV = 512
RET_CHUNK = 128
RET_ROPE_THETA = 10000.0

MOBA_HEADS = 8
MOBA_HEAD_DIM = 128
MOBA_BLOCK = 256
MOBA_TOPK = 3
MOBA_ROPE_THETA = 500000.0
MOBA_ROT_DIM = MOBA_HEAD_DIM // 4
MOBA_SCALE = MOBA_HEAD_DIM ** -0.5

PEER_HEADS = 8
PEER_N_KEYS = 128
PEER_KEY_DIM = 256
PEER_TOPK = 16

LANES = 128
V7X_VMEM_BYTES = 64 * 1024 * 1024
BIG_VMEM_LIMIT = V7X_VMEM_BYTES * 3 // 4
NEG = -1e30
INV_SQRT2 = 0.7071067811865476

_NT = (((1,), (1,)), ((), ()))


def _cp(sem, vmem=None):
    kw = dict(dimension_semantics=sem)
    if vmem is not None:
        kw["vmem_limit_bytes"] = vmem
    return pltpu.CompilerParams(**kw)


def _tok_spec(tm, f):
    return pl.BlockSpec((tm, f), lambda i, *_: (i, 0))


def _mod_spec(mod, n, tm):
    g, r, d = mod.shape
    tiles_per_group = n // (g * tm)
    return pl.BlockSpec((1, r, d), lambda i, *_: (i // tiles_per_group, 0, 0))


def _tab_spec(tab, tm):
    p, w = tab.shape
    if p == 1:
        return pl.BlockSpec((1, w), lambda i, *_: (0, 0))
    nt = p // tm
    return pl.BlockSpec((tm, w), lambda i, *_: (i % nt, 0))


def _full_spec(a):
    nd = a.ndim
    return pl.BlockSpec(a.shape, lambda *_: (0,) * nd)


def _rope_tables(pos, theta, rot_dim, period, reps):
    half = rot_dim // 2
    freqs = jnp.exp(-math.log(theta) * jnp.arange(half, dtype=F32) * (2.0 / rot_dim))
    ang = pos.astype(F32)[:, None] * freqs[None, :]
    cos, sin = jnp.cos(ang), jnp.sin(ang)
    p = pos.shape[0]
    rest = period - rot_dim
    c = jnp.concatenate([cos, cos, jnp.ones((p, rest), F32)], axis=1)
    sa = jnp.concatenate([-sin, jnp.zeros((p, period - half), F32)], axis=1)
    sb = jnp.concatenate([jnp.zeros((p, half), F32), sin, jnp.zeros((p, rest), F32)], axis=1)
    return tuple(jnp.tile(t, (1, reps)) for t in (c, sa, sb))


def _rope(x, c, sa, sb, half):
    w = x.shape[-1]
    return x * c + pltpu.roll(x, w - half, 1) * sa + pltpu.roll(x, half, 1) * sb


def _post_norm(x, f, gate, g, b, alpha):
    y = alpha * x + (1.0 + gate) * f
    mu = jnp.mean(y, axis=-1, keepdims=True)
    yc = y - mu
    var = jnp.mean(yc * yc, axis=-1, keepdims=True)
    return yc * lax.rsqrt(var + LN_EPS) * g + b


def _rms(x):
    return x * lax.rsqrt(jnp.mean(x * x, axis=-1, keepdims=True) + RMS_EPS)


def _adaln_kernel(c_ref, w_ref, b_ref, o_ref):
    c = c_ref[...]
    a = (c * jax.nn.sigmoid(c)).astype(BF16)
    o_ref[0] = jnp.dot(a, w_ref[0].astype(BF16), preferred_element_type=F32) + b_ref[0]


def _adaln(c_all, ada_w, ada_b):
    nl = ada_w.shape[0] * ada_w.shape[1]
    d, f = ada_w.shape[2], ada_w.shape[3]
    w = ada_w.reshape(nl, d, f)
    b = ada_b.reshape(nl, 1, f)
    r = c_all.shape[0]
    tn = 1024
    return pl.pallas_call(
        _adaln_kernel,
        grid=(nl, f // tn),
        in_specs=[pl.BlockSpec((r, d), lambda l, j: (0, 0)),
                  pl.BlockSpec((1, d, tn), lambda l, j: (l, 0, j)),
                  pl.BlockSpec((1, 1, tn), lambda l, j: (l, 0, j))],
        out_specs=pl.BlockSpec((1, r, tn), lambda l, j: (l, 0, j)),
        out_shape=jax.ShapeDtypeStruct((nl, r, f), F32),
        compiler_params=_cp(("parallel", "parallel")),
        name="adaln",
    )(c_all, w, b)


def _mla_proj_kernel(x_ref, sh_ref, sc_ref, win_ref, qn_ref, kvn_ref, wuq_ref, wuk_ref,
                     cq_ref, sqa_ref, sqb_ref, ck_ref, ska_ref, skb_ref,
                     ckv_ref, kpe_ref, kcat_ref, vt_ref, qcat_ref):
    h = (x_ref[...] * (1.0 + sc_ref[0]) + sh_ref[0]).astype(BF16)
    z = jnp.dot(h, win_ref[...], preferred_element_type=F32)
    cq = _rms(z[:, :MLA_Q_RANK]) * qn_ref[...]
    ckv = _rms(z[:, MLA_Q_RANK:MLA_Q_RANK + MLA_KV_RANK]) * kvn_ref[...]
    kx = z[:, MLA_Q_RANK + MLA_KV_RANK:]
    kpe = _rope(kx, ck_ref[...], ska_ref[...], skb_ref[...], MLA_ROPE // 2)
    ckv_ref[...] = ckv
    kpe_ref[...] = kpe[:, :MLA_ROPE]
    kcat_ref[:, :MLA_KV_RANK] = ckv.astype(BF16)
    kcat_ref[:, MLA_KV_RANK:] = kpe.astype(BF16)
    vt_ref[0] = ckv.T.astype(BF16)
    q = jnp.dot(cq.astype(BF16), wuq_ref[...], preferred_element_type=F32)
    qx = q[:, MLA_HEADS * MLA_NOPE:]
    qp = _rope(qx, cq_ref[...], sqa_ref[...], sqb_ref[...], MLA_ROPE // 2) * MLA_SCALE
    zero = jnp.zeros((qp.shape[0], LANES - MLA_ROPE), F32)
    for hh in range(MLA_HEADS):
        qn = q[:, hh * MLA_NOPE:(hh + 1) * MLA_NOPE].astype(BF16)
        ql = jnp.dot(qn, wuk_ref[hh], preferred_element_type=F32) * MLA_SCALE
        base = hh * MLA_QK_WIDTH
        qcat_ref[:, base:base + MLA_KV_RANK] = ql.astype(BF16)
        qph = jnp.concatenate([qp[:, hh * MLA_ROPE:(hh + 1) * MLA_ROPE], zero], axis=1)
        qcat_ref[:, base + MLA_KV_RANK:base + MLA_QK_WIDTH] = qph.astype(BF16)


def _mla_proj(x, sh, sc, w, tabs_q, tabs_k, b, s, tm):
    n, d = x.shape
    hd = MLA_HEADS
    nt = s // tm
    ins = [x, sh, sc, w["w_in"], w["q_norm"], w["kv_norm"], w["w_uq"], w["w_uk"], *tabs_q, *tabs_k]
    in_specs = ([_tok_spec(tm, d), _mod_spec(sh, n, tm), _mod_spec(sc, n, tm)]
                + [_full_spec(a) for a in ins[3:8]]
                + [_tab_spec(t, tm) for t in ins[8:]])
    return pl.pallas_call(
        _mla_proj_kernel,
        grid=(n // tm,),
        in_specs=in_specs,
        out_specs=[_tok_spec(tm, MLA_KV_RANK), _tok_spec(tm, MLA_ROPE), _tok_spec(tm, MLA_QK_WIDTH),
                   pl.BlockSpec((1, MLA_KV_RANK, tm), lambda i: (i // nt, 0, i % nt)),
                   _tok_spec(tm, hd * MLA_QK_WIDTH)],
        out_shape=[jax.ShapeDtypeStruct((n, MLA_KV_RANK), F32), jax.ShapeDtypeStruct((n, MLA_ROPE), F32),
                   jax.ShapeDtypeStruct((n, MLA_QK_WIDTH), BF16),
                   jax.ShapeDtypeStruct((b, MLA_KV_RANK, s), BF16),
                   jax.ShapeDtypeStruct((n, hd * MLA_QK_WIDTH), BF16)],
        compiler_params=_cp(("parallel",), BIG_VMEM_LIMIT),
        name="mla_proj",
    )(*ins)


def _fold_causal(p, t, nq, count):
    n_lo = count(p)
    is_lo = t < n_lo
    return jnp.where(is_lo, p, nq - 1 - p), jnp.where(is_lo, t, t - n_lo)


def _mla_attn_kernel(q_ref, k_ref, vt_ref, o_ref, m_sc, l_sc, acc_sc, *, tq, tk, nq):
    i, j = _fold_causal(pl.program_id(1), pl.program_id(2), nq, lambda q: (q * tq + tq - 1) // tk + 1)
    hd, r, w = MLA_HEADS, MLA_KV_RANK, MLA_QK_WIDTH

    @pl.when(j == 0)
    def _():
        m_sc[...] = jnp.full((hd, 1, tq), NEG, F32)
        l_sc[...] = jnp.zeros((hd, 1, tq), F32)
        acc_sc[...] = jnp.zeros((hd, r, tq), F32)

    def step(masked):
        k = k_ref[...]
        vt = vt_ref[0]
        if masked:
            kpos = j * tk + lax.broadcasted_iota(jnp.int32, (tk, tq), 0)
            qpos = i * tq + lax.broadcasted_iota(jnp.int32, (tk, tq), 1)
            keep = kpos <= qpos
        for hh in range(hd):
            st = lax.dot_general(k, q_ref[:, hh * w:(hh + 1) * w], _NT, preferred_element_type=F32)
            if masked:
                st = jnp.where(keep, st, NEG)
            m_prev = m_sc[hh]
            m_new = jnp.maximum(m_prev, jnp.max(st, axis=0, keepdims=True))
            alpha = jnp.exp(m_prev - m_new)
            p = jnp.exp(st - m_new)
            l_sc[hh] = alpha * l_sc[hh] + jnp.sum(p, axis=0, keepdims=True)
            acc_sc[hh] = alpha * acc_sc[hh] + jnp.dot(vt, p.astype(BF16), preferred_element_type=F32)
            m_sc[hh] = m_new

    crosses_diagonal = j * tk + tk - 1 > i * tq

    @pl.when(crosses_diagonal)
    def _():
        step(True)

    @pl.when(jnp.logical_not(crosses_diagonal))
    def _():
        step(False)

    @pl.when(j == (i * tq + tq - 1) // tk)
    def _():
        for hh in range(hd):
            o_ref[:, hh * r:(hh + 1) * r] = (acc_sc[hh] / l_sc[hh]).T.astype(o_ref.dtype)


def _mla_attn(q, kcat, vt, b, s):
    hd, r, w = MLA_HEADS, MLA_KV_RANK, MLA_QK_WIDTH
    tq = min(256, s)
    tk = min(512, s)
    nq, nk = s // tq, s // tk

    def count(i):
        return (i * tq + tq - 1) // tk + 1

    steps = count(0) + count(nq - 1)
    assert nq % 2 == 0 and all(count(p) + count(nq - 1 - p) == steps for p in range(nq // 2))

    def qmap(bb, p, t):
        return (bb * nq + _fold_causal(p, t, nq, count)[0], 0)

    return pl.pallas_call(
        functools.partial(_mla_attn_kernel, tq=tq, tk=tk, nq=nq),
        grid=(b, nq // 2, steps),
        in_specs=[pl.BlockSpec((tq, hd * w), qmap),
                  pl.BlockSpec((tk, w), lambda bb, p, t: (bb * nk + _fold_causal(p, t, nq, count)[1], 0)),
                  pl.BlockSpec((1, r, tk), lambda bb, p, t: (bb, 0, _fold_causal(p, t, nq, count)[1]))],
        out_specs=pl.BlockSpec((tq, hd * r), qmap),
        out_shape=jax.ShapeDtypeStruct((b * s, hd * r), BF16),
        scratch_shapes=[pltpu.VMEM((hd, 1, tq), F32), pltpu.VMEM((hd, 1, tq), F32),
                        pltpu.VMEM((hd, r, tq), F32)],
        compiler_params=_cp(("parallel", "parallel", "arbitrary"), BIG_VMEM_LIMIT),
        name="mla_attn",
    )(q, kcat, vt)


MLA_DEC_ROWS = 16
MLA_DEC_PAGES = 16


def _mla_dec_kernel(pt_ref, ql_ref, qp_ref, kn_ref, kpn_ref, *rest, g_pages):
    pages = rest[:g_pages]
    ppages = rest[g_pages:2 * g_pages]
    o_ref = rest[2 * g_pages]
    m_sc, l_sc, acc_sc = rest[2 * g_pages + 1:]
    g = pl.program_id(1)
    ql = ql_ref[0]
    qp = qp_ref[0]

    @pl.when(g == 0)
    def _():
        kn = kn_ref[0]
        s_self = (jnp.sum(ql.astype(F32) * kn, axis=1, keepdims=True)
                  + jnp.sum(qp.astype(F32) * kpn_ref[0], axis=1, keepdims=True))
        m_sc[...] = s_self
        l_sc[...] = jnp.ones_like(s_self)
        acc_sc[...] = jnp.broadcast_to(kn, acc_sc.shape)

    kbs = []
    ss = []
    for t in range(g_pages):
        kb = pages[t][0, 0].astype(BF16)
        kpt = ppages[t][0, 0].astype(BF16)
        kbs.append(kb)
        ss.append(lax.dot_general(ql, kb, _NT, preferred_element_type=F32)
                  + jnp.dot(qp, kpt, preferred_element_type=F32))
    s = jnp.concatenate(ss, axis=1)
    m_prev = m_sc[...]
    m_new = jnp.maximum(m_prev, jnp.max(s, axis=1, keepdims=True))
    alpha = jnp.exp(m_prev - m_new)
    p = jnp.exp(s - m_new)
    l_sc[...] = alpha * l_sc[...] + jnp.sum(p, axis=1, keepdims=True)
    acc = alpha * acc_sc[...]
    for t in range(g_pages):
        acc = acc + jnp.dot(p[:, t * PAGE_SIZE:(t + 1) * PAGE_SIZE].astype(BF16), kbs[t],
                            preferred_element_type=F32)
    acc_sc[...] = acc
    m_sc[...] = m_new

    @pl.when(g == pl.num_programs(1) - 1)
    def _():
        o_ref[0] = (acc_sc[...] / l_sc[...]).astype(o_ref.dtype)


def _mla_dec(page_table, ql, qp, kn, kpn, cache_ckv, cache_kpe_t, layer):
    db, n_pages = page_table.shape
    gp = math.gcd(n_pages, MLA_DEC_PAGES)
    rows = MLA_DEC_ROWS

    def page_spec(shape, t):
        return pl.BlockSpec((1, 1) + shape, lambda bb, g, pt: (layer, pt[bb, g * gp + t], 0, 0))

    in_specs = ([pl.BlockSpec((1, rows, MLA_KV_RANK), lambda bb, g, pt: (bb, 0, 0)),
                 pl.BlockSpec((1, rows, MLA_ROPE), lambda bb, g, pt: (bb, 0, 0)),
                 pl.BlockSpec((1, 1, MLA_KV_RANK), lambda bb, g, pt: (bb, 0, 0)),
                 pl.BlockSpec((1, 1, MLA_ROPE), lambda bb, g, pt: (bb, 0, 0))]
                + [page_spec((PAGE_SIZE, MLA_KV_RANK), t) for t in range(gp)]
                + [page_spec((MLA_ROPE, PAGE_SIZE), t) for t in range(gp)])
    return pl.pallas_call(
        functools.partial(_mla_dec_kernel, g_pages=gp),
        grid_spec=pltpu.PrefetchScalarGridSpec(
            num_scalar_prefetch=1,
            grid=(db, n_pages // gp),
            in_specs=in_specs,
            out_specs=pl.BlockSpec((1, rows, MLA_KV_RANK), lambda bb, g, pt: (bb, 0, 0)),
            scratch_shapes=[pltpu.VMEM((rows, 1), F32), pltpu.VMEM((rows, 1), F32),
                            pltpu.VMEM((rows, MLA_KV_RANK), F32)]),
        out_shape=jax.ShapeDtypeStruct((db, rows, MLA_KV_RANK), BF16),
        compiler_params=_cp(("parallel", "arbitrary")),
        name="mla_dec",
    )(page_table, ql, qp, kn, kpn, *([cache_ckv] * gp), *([cache_kpe_t] * gp))


def _mla_out_kernel(o_ref, wuv_ref, wo_ref, x_ref, gt_ref, g_ref, b_ref, y_ref, t_sc, *, alpha):
    for hh in range(MLA_HEADS):
        t = jnp.dot(o_ref[:, hh * MLA_KV_RANK:(hh + 1) * MLA_KV_RANK], wuv_ref[hh],
                    preferred_element_type=F32)
        t_sc[:, hh * MLA_VDIM:(hh + 1) * MLA_VDIM] = t.astype(BF16)
    f = jnp.dot(t_sc[...], wo_ref[...], preferred_element_type=F32)
    y_ref[...] = _post_norm(x_ref[...], f, gt_ref[0], g_ref[...], b_ref[...], alpha)


def _mla_out(o, w, x, gate, ln_g, ln_b, alpha, tm):
    n, d = x.shape
    ins = [o, w["w_uv"], w["w_o"], x, gate, ln_g, ln_b]
    return pl.pallas_call(
        functools.partial(_mla_out_kernel, alpha=alpha),
        grid=(n // tm,),
        in_specs=[_tok_spec(tm, o.shape[1]), _full_spec(ins[1]), _full_spec(ins[2]),
                  _tok_spec(tm, d), _mod_spec(gate, n, tm), _full_spec(ln_g), _full_spec(ln_b)],
        out_specs=_tok_spec(tm, d),
        out_shape=jax.ShapeDtypeStruct((n, d), F32),
        scratch_shapes=[pltpu.VMEM((tm, MLA_HEADS * MLA_VDIM), BF16)],
        compiler_params=_cp(("parallel",), BIG_VMEM_LIMIT),
        name="mla_out",
    )(*ins)


RET_COLS = 1024


def _ret_proj_kernel(x_ref, sh_ref, sc_ref, w_ref, c_ref, sa_ref, sb_ref, z_ref):
    j = pl.program_id(1)
    h = (x_ref[...] * (1.0 + sc_ref[0]) + sh_ref[0]).astype(BF16)
    z = jnp.dot(h, w_ref[...], preferred_element_type=F32)

    @pl.when(j < 2)
    def _():
        kscale = jnp.where(j == 1, RET_DK ** -0.5, 1.0).astype(F32)
        z_ref[...] = _rope(z, c_ref[...], sa_ref[...], sb_ref[...], RET_DK // 2) * kscale

    @pl.when(j >= 2)
    def _():
        z_ref[...] = z


def _ret_proj(x, sh, sc, w_in, tabs, tm):
    n, d = x.shape
    f = w_in.shape[1]
    assert RET_HEADS * RET_DK == RET_COLS
    return pl.pallas_call(
        _ret_proj_kernel,
        grid=(n // tm, f // RET_COLS),
        in_specs=[_tok_spec(tm, d), _mod_spec(sh, n, tm), _mod_spec(sc, n, tm),
                  pl.BlockSpec((d, RET_COLS), lambda i, j: (0, j))]
                 + [_tab_spec(t, tm) for t in tabs],
        out_specs=pl.BlockSpec((tm, RET_COLS), lambda i, j: (i, j)),
        out_shape=jax.ShapeDtypeStruct((n, f), F32),
        compiler_params=_cp(("parallel", "arbitrary"), BIG_VMEM_LIMIT),
        name="ret_proj",
    )(x, sh, sc, w_in, *tabs)


def _ret_decay_tables():
    ln = RET_CHUNK
    lg = jnp.log1p(-jnp.exp2(-5.0 - jnp.arange(RET_HEADS, dtype=F32)))
    idx = jnp.arange(ln, dtype=F32)
    diff = idx[:, None] - idx[None, :]
    decay = jnp.where(diff >= 0, jnp.exp(jnp.maximum(diff, 0.0)[None] * lg[:, None, None]), 0.0)
    rs = jnp.exp((idx[None, :] + 1.0) * lg[:, None])[:, :, None]
    wk = jnp.exp((ln - 1.0 - idx)[None, :] * lg[:, None])[:, :, None]
    gl = jnp.exp(ln * lg)[:, None, None]
    g1 = jnp.exp(lg)[:, None, None]
    return decay, rs, wk, gl, g1


def _ret_chunk_kernel(q_ref, k_ref, v_ref, dec_ref, rs_ref, wk_ref, gl_ref, o_ref, sf_ref, s_sc):
    c = pl.program_id(2)

    @pl.when(c == 0)
    def _():
        s_sc[...] = jnp.zeros_like(s_sc)

    q = q_ref[...].astype(BF16)
    k = k_ref[...]
    v = v_ref[...].astype(BF16)
    s_old = s_sc[...]
    inner = lax.dot_general(q, k.astype(BF16), _NT, preferred_element_type=F32) * dec_ref[0]
    o = jnp.dot(inner.astype(BF16), v, preferred_element_type=F32)
    o = o + jnp.dot(q, s_old.astype(BF16), preferred_element_type=F32) * rs_ref[0]
    o_ref[...] = o
    kw_t = (k * wk_ref[0]).T.astype(BF16)
    s_new = gl_ref[0] * s_old + jnp.dot(kw_t, v, preferred_element_type=F32)
    s_sc[...] = s_new

    @pl.when(c == pl.num_programs(2) - 1)
    def _():
        sf_ref[0, 0] = s_new


def _ret_chunks(z, b, s, tables):
    decay, rs, wk, gl, _ = tables
    ln = RET_CHUNK
    nc = s // ln
    hk = RET_HEADS
    kv0 = 2 * hk * RET_DK // RET_DV
    return pl.pallas_call(
        _ret_chunk_kernel,
        grid=(b, hk, nc),
        in_specs=[pl.BlockSpec((ln, RET_DK), lambda bb, hh, c: (bb * nc + c, hh)),
                  pl.BlockSpec((ln, RET_DK), lambda bb, hh, c: (bb * nc + c, hk + hh)),
                  pl.BlockSpec((ln, RET_DV), lambda bb, hh, c: (bb * nc + c, kv0 + hh)),
                  pl.BlockSpec((1, ln, ln), lambda bb, hh, c: (hh, 0, 0)),
                  pl.BlockSpec((1, ln, 1), lambda bb, hh, c: (hh, 0, 0)),
                  pl.BlockSpec((1, ln, 1), lambda bb, hh, c: (hh, 0, 0)),
                  pl.BlockSpec((1, 1, 1), lambda bb, hh, c: (hh, 0, 0))],
        out_specs=[pl.BlockSpec((ln, RET_DV), lambda bb, hh, c: (bb * nc + c, hh)),
                   pl.BlockSpec((1, 1, RET_DK, RET_DV), lambda bb, hh, c: (bb, hh, 0, 0))],
        out_shape=[jax.ShapeDtypeStruct((b * s, hk * RET_DV), F32),
                   jax.ShapeDtypeStruct((b, hk, RET_DK, RET_DV), F32)],
        scratch_shapes=[pltpu.VMEM((RET_DK, RET_DV), F32)],
        compiler_params=_cp(("parallel", "parallel", "arbitrary")),
        name="ret_chunks",
    )(z, z, z, decay, rs, wk, gl)


def _ret_step_kernel(s_ref, q_ref, k_ref, v_ref, g1_ref, o_ref, sn_ref):
    s_old = s_ref[0, 0]
    qc = q_ref[0, 0]
    kc = k_ref[0, 0]
    v = v_ref[0, 0]
    gamma = g1_ref[0]
    inner = jnp.sum(qc * kc, axis=0, keepdims=True)
    o_ref[0, 0] = inner * v + jnp.sum(qc * s_old, axis=0, keepdims=True) * gamma
    sn_ref[0, 0] = gamma * s_old + kc * v


def _ret_step(state, z, g1):
    db = state.shape[0]
    hk = RET_HEADS
    q = z[:, :hk * RET_DK].reshape(db, hk, RET_DK, 1)
    k = z[:, hk * RET_DK:2 * hk * RET_DK].reshape(db, hk, RET_DK, 1)
    v = z[:, 2 * hk * RET_DK:2 * hk * RET_DK + hk * RET_DV].reshape(db, hk, 1, RET_DV)
    col = pl.BlockSpec((1, 1, RET_DK, 1), lambda bb, hh: (bb, hh, 0, 0))
    row = pl.BlockSpec((1, 1, 1, RET_DV), lambda bb, hh: (bb, hh, 0, 0))
    st = pl.BlockSpec((1, 1, RET_DK, RET_DV), lambda bb, hh: (bb, hh, 0, 0))
    o, s_new = pl.pallas_call(
        _ret_step_kernel,
        grid=(db, hk),
        in_specs=[st, col, col, row, pl.BlockSpec((1, 1, 1), lambda bb, hh: (hh, 0, 0))],
        out_specs=[row, st],
        out_shape=[jax.ShapeDtypeStruct((db, hk, 1, RET_DV), F32),
                   jax.ShapeDtypeStruct(state.shape, F32)],
        compiler_params=_cp(("parallel", "parallel")),
        name="ret_step",
    )(state, q, k, v, g1)
    return o.reshape(db, hk * RET_DV), s_new


def _ret_out_kernel(o_ref, gz_ref, gn_ref, wo_ref, x_ref, gt_ref, g_ref, b_ref, y_ref, t_sc, *, alpha):
    for hh in range(RET_HEADS):
        oh = o_ref[:, hh * RET_DV:(hh + 1) * RET_DV]
        mu = jnp.mean(oh, axis=-1, keepdims=True)
        oc = oh - mu
        var = jnp.mean(oc * oc, axis=-1, keepdims=True)
        yh = oc * lax.rsqrt(var + LN_EPS) * gn_ref[:, hh * RET_DV:(hh + 1) * RET_DV]
        gz = gz_ref[:, hh * RET_DV:(hh + 1) * RET_DV]
        t_sc[:, hh * RET_DV:(hh + 1) * RET_DV] = (gz * jax.nn.sigmoid(gz) * yh).astype(BF16)
    f = jnp.dot(t_sc[...], wo_ref[...], preferred_element_type=F32)
    y_ref[...] = _post_norm(x_ref[...], f, gt_ref[0], g_ref[...], b_ref[...], alpha)


def _ret_out(o, z, gn, w_o, x, gate, ln_g, ln_b, alpha, tm):
    n, d = x.shape
    hv = RET_HEADS * RET_DV
    gblk = z.shape[1] // hv - 1
    return pl.pallas_call(
        functools.partial(_ret_out_kernel, alpha=alpha),
        grid=(n // tm,),
        in_specs=[_tok_spec(tm, hv), pl.BlockSpec((tm, hv), lambda i: (i, gblk)),
                  _full_spec(gn), _full_spec(w_o), _tok_spec(tm, d), _mod_spec(gate, n, tm),
                  _full_spec(ln_g), _full_spec(ln_b)],
        out_specs=_tok_spec(tm, d),
        out_shape=jax.ShapeDtypeStruct((n, d), F32),
        scratch_shapes=[pltpu.VMEM((tm, hv), BF16)],
        compiler_params=_cp(("parallel",), BIG_VMEM_LIMIT),
        name="ret_out",
    )(o, z, gn, w_o, x, gate, ln_g, ln_b)


def _moba_proj_kernel(x_ref, sh_ref, sc_ref, w_ref, c_ref, sa_ref, sb_ref,
                      q_ref, k_ref, v_ref, kb_ref, vt_ref, km_ref):
    hd, dh = MOBA_HEADS, MOBA_HEAD_DIM
    h = (x_ref[...] * (1.0 + sc_ref[0]) + sh_ref[0]).astype(BF16)
    z = jnp.dot(h, w_ref[...], preferred_element_type=F32)
    c, sa, sb = c_ref[...], sa_ref[...], sb_ref[...]
    q = _rope(z[:, :hd * dh], c, sa, sb, MOBA_ROT_DIM // 2)
    k = _rope(z[:, hd * dh:2 * hd * dh], c, sa, sb, MOBA_ROT_DIM // 2)
    q_ref[...] = q.astype(BF16)
    for hh in range(hd):
        kh = k[:, hh * dh:(hh + 1) * dh]
        vh = z[:, (2 * hd + hh) * dh:(2 * hd + hh + 1) * dh]
        k_ref[0, hh] = kh
        v_ref[0, hh] = vh
        kb_ref[0, hh] = kh.astype(BF16)
        vt_ref[0, hh] = vh.T.astype(BF16)
        km_ref[0, 0, hh:hh + 1, :] = jnp.mean(kh, axis=0, keepdims=True)


def _moba_proj(x, sh, sc, w_qkv, tabs, b, s, tm):
    n, d = x.shape
    hd, dh = MOBA_HEADS, MOBA_HEAD_DIM
    nt = s // tm
    kv_spec = pl.BlockSpec((1, hd, tm, dh), lambda i: (i // nt, 0, i % nt, 0))
    kv_shape = (b, hd, s, dh)
    return pl.pallas_call(
        _moba_proj_kernel,
        grid=(n // tm,),
        in_specs=[_tok_spec(tm, d), _mod_spec(sh, n, tm), _mod_spec(sc, n, tm), _full_spec(w_qkv)]
                 + [_tab_spec(t, tm) for t in tabs],
        out_specs=[_tok_spec(tm, hd * dh), kv_spec, kv_spec, kv_spec,
                   pl.BlockSpec((1, hd, dh, tm), lambda i: (i // nt, 0, 0, i % nt)),
                   pl.BlockSpec((1, 1, hd, dh), lambda i: (i // nt, i % nt, 0, 0))],
        out_shape=[jax.ShapeDtypeStruct((n, hd * dh), BF16),
                   jax.ShapeDtypeStruct(kv_shape, F32), jax.ShapeDtypeStruct(kv_shape, F32),
                   jax.ShapeDtypeStruct(kv_shape, BF16), jax.ShapeDtypeStruct((b, hd, dh, s), BF16),
                   jax.ShapeDtypeStruct((b, nt, hd, dh), F32)],
        compiler_params=_cp(("parallel",), BIG_VMEM_LIMIT),
        name="moba_proj",
    )(x, sh, sc, w_qkv, *tabs)


def _moba_attn_kernel(q_ref, k_ref, vt_ref, km_ref, o_ref, m_sc, l_sc, acc_sc, sel_sc, *, nb):
    hd, dh, blk = MOBA_HEADS, MOBA_HEAD_DIM, MOBA_BLOCK
    i, jj = _fold_causal(pl.program_id(1), pl.program_id(2), nb, lambda q: q + 1)

    def scores(hh):
        q = q_ref[:, hh * dh:(hh + 1) * dh]
        return lax.dot_general(k_ref[0, hh], q, _NT, preferred_element_type=F32) * MOBA_SCALE

    @pl.when(jj == 0)
    def _():
        rowb = lax.broadcasted_iota(jnp.int32, (LANES, blk), 0)
        rowbf = rowb.astype(F32)
        kpos = lax.broadcasted_iota(jnp.int32, (blk, blk), 0)
        qpos = lax.broadcasted_iota(jnp.int32, (blk, blk), 1)
        for hh in range(hd):
            km = km_ref[0, hh].astype(BF16)
            gate = lax.dot_general(km, q_ref[:, hh * dh:(hh + 1) * dh], _NT, preferred_element_type=F32)
            gate = jnp.where(rowb < i, gate, -jnp.inf)
            sel = jnp.zeros((LANES, blk), F32)
            for _ in range(min(MOBA_TOPK, nb)):
                mx = jnp.max(gate, axis=0, keepdims=True)
                idx = jnp.min(jnp.where(gate == mx, rowbf, float(LANES)), axis=0, keepdims=True)
                hit = rowbf == idx
                sel = jnp.where(hit & (rowb < i), 1.0, sel)
                gate = jnp.where(hit, -jnp.inf, gate)
            for n in range(nb):
                sel_sc[hh, n] = sel[n:n + 1, :]
            st = jnp.where(kpos <= qpos, scores(hh), NEG)
            mx = jnp.max(st, axis=0, keepdims=True)
            p = jnp.exp(st - mx)
            m_sc[hh] = mx
            l_sc[hh] = jnp.sum(p, axis=0, keepdims=True)
            acc_sc[hh] = jnp.dot(vt_ref[0, hh], p.astype(BF16), preferred_element_type=F32)

    @pl.when(jj > 0)
    def _():
        for hh in range(hd):
            picked = sel_sc[hh, jj - 1]
            st = jnp.where(picked > 0.5, scores(hh), NEG)
            m_prev = m_sc[hh]
            m_new = jnp.maximum(m_prev, jnp.max(st, axis=0, keepdims=True))
            alpha = jnp.exp(m_prev - m_new)
            p = jnp.exp(st - m_new)
            l_sc[hh] = alpha * l_sc[hh] + jnp.sum(p, axis=0, keepdims=True)
            acc_sc[hh] = alpha * acc_sc[hh] + jnp.dot(vt_ref[0, hh], p.astype(BF16),
                                                      preferred_element_type=F32)
            m_sc[hh] = m_new

    @pl.when(jj == i)
    def _():
        for hh in range(hd):
            o_ref[:, hh * dh:(hh + 1) * dh] = (acc_sc[hh] / l_sc[hh]).T.astype(o_ref.dtype)


def _moba_attn(q, kb, vt, kmean, b, s):
    hd, dh, blk = MOBA_HEADS, MOBA_HEAD_DIM, MOBA_BLOCK
    nb = s // blk
    assert nb <= LANES and nb % 2 == 0

    def count(i):
        return i + 1

    def own_and_step(p, t):
        return _fold_causal(p, t, nb, count)

    def kv_block(p, t):
        i, jj = own_and_step(p, t)
        return jnp.where(jj == 0, i, jj - 1)

    qmap = lambda bb, p, t: (bb * nb + own_and_step(p, t)[0], 0)
    return pl.pallas_call(
        functools.partial(_moba_attn_kernel, nb=nb),
        grid=(b, nb // 2, nb + 1),
        in_specs=[pl.BlockSpec((blk, hd * dh), qmap),
                  pl.BlockSpec((1, hd, blk, dh), lambda bb, p, t: (bb, 0, kv_block(p, t), 0)),
                  pl.BlockSpec((1, hd, dh, blk), lambda bb, p, t: (bb, 0, 0, kv_block(p, t))),
                  pl.BlockSpec((1, hd, LANES, dh), lambda bb, p, t: (bb, 0, 0, 0))],
        out_specs=pl.BlockSpec((blk, hd * dh), qmap),
        out_shape=jax.ShapeDtypeStruct((b * s, hd * dh), BF16),
        scratch_shapes=[pltpu.VMEM((hd, 1, blk), F32), pltpu.VMEM((hd, 1, blk), F32),
                        pltpu.VMEM((hd, dh, blk), F32), pltpu.VMEM((hd, nb, 1, blk), F32)],
        compiler_params=_cp(("parallel", "parallel", "arbitrary"), BIG_VMEM_LIMIT),
        name="moba_attn",
    )(q, kb, vt, kmean)


MOBA_DEC_PAGES = 8


def _moba_dec_mean_kernel(pt_ref, q_ref, *rest, ppb, own):
    gp = MOBA_DEC_PAGES
    pages = rest[:gp]
    sel_ref = rest[gp]
    km_sc = rest[gp + 1]
    g = pl.program_id(1)
    inv = 1.0 / (ppb * PAGE_SIZE)
    for t in range(0, gp, ppb):
        acc = jnp.sum(pages[t][0, 0], axis=1)
        for r in range(1, ppb):
            acc = acc + jnp.sum(pages[t + r][0, 0], axis=1)
        km_sc[g * (gp // ppb) + t // ppb] = acc * inv

    @pl.when(g == pl.num_programs(1) - 1)
    def _():
        gate = jnp.sum(km_sc[...] * q_ref[0][None], axis=-1)
        rowf = lax.broadcasted_iota(jnp.int32, gate.shape, 0).astype(F32)
        for t in range(min(MOBA_TOPK, own)):
            mx = jnp.max(gate, axis=0, keepdims=True)
            idx = jnp.min(jnp.where(gate == mx, rowf, float(own)), axis=0, keepdims=True)
            sel_ref[0, t:t + 1, :] = idx.astype(jnp.int32)
            gate = jnp.where(rowf == idx, -jnp.inf, gate)


def _moba_dec_select(page_table, q, pool_k, layer):
    db, n_pages = page_table.shape
    hd, dh = MOBA_HEADS, MOBA_HEAD_DIM
    ppb = MOBA_BLOCK // PAGE_SIZE
    own = n_pages // ppb
    gp = MOBA_DEC_PAGES
    assert gp % ppb == 0 and (own * ppb) % gp == 0
    topk = min(MOBA_TOPK, own)

    def page_spec(t):
        return pl.BlockSpec((1, 1, hd, PAGE_SIZE, dh),
                            lambda bb, g, pt: (layer, pt[bb, g * gp + t], 0, 0, 0))

    return pl.pallas_call(
        functools.partial(_moba_dec_mean_kernel, ppb=ppb, own=own),
        grid_spec=pltpu.PrefetchScalarGridSpec(
            num_scalar_prefetch=1,
            grid=(db, own * ppb // gp),
            in_specs=[pl.BlockSpec((1, hd, dh), lambda bb, g, pt: (bb, 0, 0))]
                     + [page_spec(t) for t in range(gp)],
            out_specs=pl.BlockSpec((1, topk, hd), lambda bb, g, pt: (bb, 0, 0)),
            scratch_shapes=[pltpu.VMEM((own, hd, dh), F32)]),
        out_shape=jax.ShapeDtypeStruct((db, topk, hd), jnp.int32),
        compiler_params=_cp(("parallel", "arbitrary")),
        name="moba_dec_select",
    )(page_table, q, *([pool_k] * gp))


def _moba_dec_attn_kernel(pt_ref, sel_ref, q_ref, kn_ref, vn_ref, *rest, npg):
    kp = rest[:npg]
    vp = rest[npg:2 * npg]
    o_ref = rest[2 * npg]
    q = q_ref[0, 0]
    s_self = jnp.sum(q * kn_ref[0, 0], axis=1, keepdims=True) * MOBA_SCALE
    ss = [jnp.sum(kp[t][0, 0, 0] * q, axis=1, keepdims=True) * MOBA_SCALE for t in range(npg)]
    mx = s_self
    for s in ss:
        mx = jnp.maximum(mx, jnp.max(s, axis=0, keepdims=True))
    p_self = jnp.exp(s_self - mx)
    den = p_self
    acc = p_self * vn_ref[0, 0]
    for t in range(npg):
        p = jnp.exp(ss[t] - mx)
        den = den + jnp.sum(p, axis=0, keepdims=True)
        acc = acc + jnp.sum(p * vp[t][0, 0, 0], axis=0, keepdims=True)
    o_ref[0, 0] = acc / den


def _moba_dec_attn(page_table, sel, q, kn, vn, pool_k, pool_v, layer):
    db = page_table.shape[0]
    hd, dh = MOBA_HEADS, MOBA_HEAD_DIM
    ppb = MOBA_BLOCK // PAGE_SIZE
    topk = sel.shape[1]
    npg = topk * ppb

    def page_spec(t):
        kk, r = t // ppb, t % ppb
        return pl.BlockSpec((1, 1, 1, PAGE_SIZE, dh),
                            lambda bb, hh, pt, sl: (layer, pt[bb, sl[bb, kk, hh] * ppb + r], hh, 0, 0))

    vec = pl.BlockSpec((1, 1, 1, dh), lambda bb, hh, pt, sl: (bb, hh, 0, 0))
    return pl.pallas_call(
        functools.partial(_moba_dec_attn_kernel, npg=npg),
        grid_spec=pltpu.PrefetchScalarGridSpec(
            num_scalar_prefetch=2,
            grid=(db, hd),
            in_specs=[vec, vec, vec] + [page_spec(t) for t in range(npg)] * 2,
            out_specs=vec),
        out_shape=jax.ShapeDtypeStruct((db, hd, 1, dh), F32),
        compiler_params=_cp(("parallel", "parallel")),
        name="moba_dec_attn",
    )(page_table, sel, q, kn, vn, *([pool_k] * npg), *([pool_v] * npg))


def _proj_out_kernel(o_ref, wo_ref, x_ref, gt_ref, g_ref, b_ref, y_ref, *, alpha):
    f = jnp.dot(o_ref[...].astype(BF16), wo_ref[...], preferred_element_type=F32)
    y_ref[...] = _post_norm(x_ref[...], f, gt_ref[0], g_ref[...], b_ref[...], alpha)


def _proj_out(o, w_o, x, gate, ln_g, ln_b, alpha, tm):
    n, d = x.shape
    return pl.pallas_call(
        functools.partial(_proj_out_kernel, alpha=alpha),
        grid=(n // tm,),
        in_specs=[_tok_spec(tm, o.shape[1]), _full_spec(w_o), _tok_spec(tm, d),
                  _mod_spec(gate, n, tm), _full_spec(ln_g), _full_spec(ln_b)],
        out_specs=_tok_spec(tm, d),
        out_shape=jax.ShapeDtypeStruct((n, d), F32),
        compiler_params=_cp(("parallel",), BIG_VMEM_LIMIT),
        name="proj_out",
    )(o, w_o, x, gate, ln_g, ln_b)


PEER_DENSE_TOKENS = 512
PEER_DENSE_KEYS = 4
_PEER_PAIRS = [(i, j) for i in range(PEER_TOPK) for j in range(PEER_TOPK) if (i + 1) * (j + 1) <= PEER_TOPK]


def _peer_topk_kernel(x_ref, sh_ref, sc_ref, wq_ref, k1_ref, k2_ref,
                      h_ref, ca_ref, e1_ref, r2_ref, e2_ref,
                      qb_sc, s1_sc, s2_sc, v1_sc, v2_sc, i1_sc, *, tm):
    nk, hd, kt = PEER_N_KEYS, PEER_HEADS, PEER_TOPK
    half = PEER_KEY_DIM // 2
    h = (x_ref[...] * (1.0 + sc_ref[0]) + sh_ref[0]).astype(BF16)
    h_ref[...] = h
    qb_sc[...] = jnp.dot(h, wq_ref[...], preferred_element_type=F32).astype(BF16)
    iota_k = lax.broadcasted_iota(jnp.int32, (nk, LANES), 0).astype(F32)
    chunks = [slice(c * LANES, (c + 1) * LANES) for c in range(tm // LANES)]

    def pick(s):
        m = jnp.max(s, axis=0, keepdims=True)
        idx = jnp.min(jnp.where(s == m, iota_k, float(nk)), axis=0, keepdims=True)
        return m, idx, iota_k == idx

    for hh in range(hd):
        s1_sc[...] = lax.dot_general(k1_ref[...], qb_sc[:, hh * 2 * half:hh * 2 * half + half], _NT,
                                     preferred_element_type=F32)
        s2_sc[...] = lax.dot_general(k2_ref[...], qb_sc[:, hh * 2 * half + half:(hh + 1) * 2 * half], _NT,
                                     preferred_element_type=F32)
        for lanes in chunks:
            def body(k, carry, lanes=lanes, hh=hh):
                s1, s2, r2 = carry
                m1, idx1, hit1 = pick(s1)
                m2, _, hit2 = pick(s2)
                v1_sc[k, hh:hh + 1, lanes] = m1
                i1_sc[k, hh:hh + 1, lanes] = idx1
                v2_sc[k, hh:hh + 1, lanes] = m2
                return (jnp.where(hit1, -jnp.inf, s1), jnp.where(hit2, -jnp.inf, s2),
                        jnp.where(hit2, k.astype(F32), r2))

            s2_0 = s2_sc[:, lanes]
            _, _, r2 = lax.fori_loop(0, kt, body, (s1_sc[:, lanes], s2_0, jnp.full((nk, LANES), float(kt), F32)))
            r2_ref[hh, :, lanes] = r2
            e2_ref[hh, :, lanes] = jnp.exp(s2_0 - v2_sc[0, hh:hh + 1, lanes])

    pos = [float(i * kt + j) for i, j in _PEER_PAIRS]
    for lanes in chunks:
        v1 = [v1_sc[i, :, lanes] for i in range(kt)]
        v2 = [v2_sc[j, :, lanes] for j in range(kt)]
        cand0 = tuple(v1[i] + v2[j] for i, j in _PEER_PAIRS)
        m0 = cand0[0]
        zero = jnp.zeros((hd, LANES), F32)

        def cbody(_, carry, m0=m0):
            cand, cnt, z = carry
            m = cand[0]
            for cv in cand[1:]:
                m = jnp.maximum(m, cv)
            pmin = jnp.full((hd, LANES), 1e9, F32)
            for cv, pp in zip(cand, pos):
                pmin = jnp.minimum(pmin, jnp.where(cv == m, pp, 1e9))
            new_cand = []
            new_cnt = list(cnt)
            for (ci, _cj), cv, pp in zip(_PEER_PAIRS, cand, pos):
                hit = pmin == pp
                new_cand.append(jnp.where(hit, -jnp.inf, cv))
                new_cnt[ci] = new_cnt[ci] + jnp.where(hit, 1.0, 0.0)
            return tuple(new_cand), tuple(new_cnt), z + jnp.exp(m - m0)

        _, cnt, z = lax.fori_loop(0, kt, cbody, (cand0, (zero,) * kt, zero))
        zi = 1.0 / z
        e1k = [jnp.exp(v1[k] - v1[0]) * zi for k in range(kt)]
        i1 = [i1_sc[k, :, lanes] for k in range(kt)]

        def abody(a, _, cnt=cnt, e1k=e1k, i1=i1, lanes=lanes):
            af = a.astype(F32)
            ca = jnp.zeros((hd, LANES), F32)
            e1 = jnp.zeros((hd, LANES), F32)
            for k in range(kt):
                hit = i1[k] == af
                ca = jnp.where(hit, cnt[k], ca)
                e1 = jnp.where(hit, e1k[k], e1)
            ca_ref[a, :, lanes] = ca
            e1_ref[a, :, lanes] = e1
            return 0

        lax.fori_loop(0, nk, abody, 0)


def _peer_topk(x, sh, sc, w_q, k1, k2, tm):
    n, d = x.shape
    nk, hd, kt = PEER_N_KEYS, PEER_HEADS, PEER_TOPK
    row_spec = pl.BlockSpec((nk, hd, tm), lambda i: (0, 0, i))
    sel_spec = pl.BlockSpec((hd, nk, tm), lambda i: (0, 0, i))
    row_shape = jax.ShapeDtypeStruct((nk, hd, n), F32)
    sel_shape = jax.ShapeDtypeStruct((hd, nk, n), F32)
    return pl.pallas_call(
        functools.partial(_peer_topk_kernel, tm=tm),
        grid=(n // tm,),
        in_specs=[_tok_spec(tm, d), _mod_spec(sh, n, tm), _mod_spec(sc, n, tm),
                  _full_spec(w_q), _full_spec(k1), _full_spec(k2)],
        out_specs=[_tok_spec(tm, d), row_spec, row_spec, sel_spec, sel_spec],
        out_shape=[jax.ShapeDtypeStruct((n, d), BF16), row_shape, row_shape, sel_shape, sel_shape],
        scratch_shapes=[pltpu.VMEM((tm, hd * PEER_KEY_DIM), BF16),
                        pltpu.VMEM((nk, tm), F32), pltpu.VMEM((nk, tm), F32),
                        pltpu.VMEM((kt, hd, tm), F32), pltpu.VMEM((kt, hd, tm), F32),
                        pltpu.VMEM((kt, hd, tm), F32)],
        compiler_params=_cp(("parallel",), BIG_VMEM_LIMIT),
        name="peer_topk",
    )(x, sh, sc, w_q, k1, k2)


def _peer_dense_kernel(h_ref, u_ref, vt_ref, ca_ref, e1_ref, r2_ref, e2_ref,
                       x_ref, gt_ref, g_ref, b_ref, y_ref, a_sc, p_sc, acc_sc, r2_sc, e2_sc,
                       *, na, tm, alpha):
    nk, hd = PEER_N_KEYS, PEER_HEADS
    j = pl.program_id(1)

    @pl.when(j == 0)
    def _():
        acc_sc[...] = jnp.zeros_like(acc_sc)
        for hh in range(hd):
            r2_sc[hh] = r2_ref[hh].astype(BF16)
            e2_sc[hh] = e2_ref[hh].astype(BF16)

    hu = lax.dot_general(u_ref[...], h_ref[...], _NT, preferred_element_type=F32)
    a_sc[...] = (0.5 * hu * (1.0 + lax.erf(hu * INV_SQRT2))).astype(BF16)
    zero = jnp.zeros((nk, LANES), BF16)
    for ai in range(na):
        rows = slice(ai * nk, (ai + 1) * nk)
        for c in range(tm // LANES):
            lanes = slice(c * LANES, (c + 1) * LANES)
            w = zero
            for hh in range(hd):
                ca = jnp.broadcast_to(ca_ref[ai, hh:hh + 1, lanes], (nk, LANES)).astype(BF16)
                e1 = jnp.broadcast_to(e1_ref[ai, hh:hh + 1, lanes], (nk, LANES)).astype(BF16)
                w = w + jnp.where(r2_sc[hh, :, lanes] < ca, e2_sc[hh, :, lanes] * e1, zero)
            p_sc[rows, lanes] = w * a_sc[rows, lanes]
    acc_sc[...] += jnp.dot(vt_ref[0], p_sc[...], preferred_element_type=F32)

    @pl.when(j == pl.num_programs(1) - 1)
    def _():
        f = acc_sc[...].T
        y_ref[...] = _post_norm(x_ref[...], f, gt_ref[0], g_ref[...], b_ref[...], alpha)


def _peer_dense(h, u, vt, sel, x, gate, ln_g, ln_b, alpha, tm, na):
    n, d = x.shape
    nk, hd = PEER_N_KEYS, PEER_HEADS
    te = na * nk
    ne = u.shape[0]
    sel_spec = pl.BlockSpec((hd, nk, tm), lambda i, j: (0, 0, i))
    row_spec = pl.BlockSpec((na, hd, tm), lambda i, j: (j, 0, i))
    ca, e1, r2, e2 = sel
    return pl.pallas_call(
        functools.partial(_peer_dense_kernel, na=na, tm=tm, alpha=alpha),
        grid=(n // tm, ne // te),
        in_specs=[_tok_spec(tm, d), pl.BlockSpec((te, d), lambda i, j: (j, 0)),
                  pl.BlockSpec((1, d, te), lambda i, j: (j, 0, 0)),
                  row_spec, row_spec, sel_spec, sel_spec,
                  _tok_spec(tm, d), _mod_spec(gate, n, tm), _full_spec(ln_g), _full_spec(ln_b)],
        out_specs=_tok_spec(tm, d),
        out_shape=jax.ShapeDtypeStruct((n, d), F32),
        scratch_shapes=[pltpu.VMEM((te, tm), BF16), pltpu.VMEM((te, tm), BF16), pltpu.VMEM((d, tm), F32),
                        pltpu.VMEM((hd, nk, tm), BF16), pltpu.VMEM((hd, nk, tm), BF16)],
        compiler_params=_cp(("parallel", "arbitrary"), BIG_VMEM_LIMIT),
        name="peer_dense",
    )(h, u, vt, ca, e1, r2, e2, x, gate, ln_g, ln_b)


def _mods(m, ls, b, db, d):
    mp = m[ls, :b]
    ms = m[ls, b:b + db]
    prompt = tuple(mp[:, None, k * d:(k + 1) * d] for k in range(3))
    sample = tuple(ms[None, :, k * d:(k + 1) * d] for k in range(3))
    return prompt, sample


def kernel(x_prompt, x_sample, cache_mla_ckv, cache_mla_kpe, cache_moba_k, cache_moba_v, state_ret, page_table, c_prompt, c_sample, ada_w, ada_b, ln_g, ln_b, mla_w_in, mla_q_norm, mla_kv_norm, mla_w_uq, mla_w_uk, mla_w_uv, mla_w_o, ret_w_in, ret_gn, ret_w_o, moba_w_qkv, moba_w_o, peer_w_q, peer_k1, peer_k2, peer_u, peer_v):
    b, s, d = x_prompt.shape
    db, t_new, _ = x_sample.shape
    assert t_new == 1
    depth = ada_w.shape[0]
    alpha = (2 * depth) ** 0.25
    n_pages = page_table.shape[1]
    past = n_pages * PAGE_SIZE
    n_p = b * s
    tm_p = min(256, s)
    tm_s = db
    assert s % MOBA_BLOCK == 0 and tm_p == MOBA_BLOCK and db % 16 == 0

    pad = (-(b + db)) % 8
    c_all = jnp.concatenate([c_prompt, c_sample, jnp.zeros((pad, d), F32)], axis=0)
    mod = _adaln(c_all, ada_w, ada_b)

    pos_p = jnp.arange(s)
    pos_s = jnp.full((1,), past)
    cache_kpe_t = jnp.swapaxes(cache_mla_kpe, 2, 3)
    ret_tabs = _ret_decay_tables()

    xp = x_prompt.reshape(n_p, d)
    xs = x_sample.reshape(db, d)
    outs = {k: [] for k in ("ckv_p", "kpe_p", "ckv_s", "kpe_s", "mk_p", "mv_p", "mk_s", "mv_s", "rs_p", "rs_s")}

    for i in range(depth):
        j = i // N_MIXERS
        kind = i % N_MIXERS
        (shp, scp, gtp), (shs, scs, gts) = _mods(mod, 2 * i, b, db, d)
        g0, b0 = ln_g[i, 0][None], ln_b[i, 0][None]
        if kind == 0:
            hd = MLA_HEADS
            w_in = jnp.pad(mla_w_in[j], ((0, 0), (0, LANES - MLA_ROPE))).astype(BF16)
            w_uq = mla_w_uq[j].reshape(MLA_Q_RANK, hd, MLA_NOPE + MLA_ROPE)
            w_uq = jnp.concatenate([w_uq[:, :, :MLA_NOPE].reshape(MLA_Q_RANK, hd * MLA_NOPE),
                                    w_uq[:, :, MLA_NOPE:].reshape(MLA_Q_RANK, hd * MLA_ROPE)], axis=1)
            w = dict(w_in=w_in, q_norm=mla_q_norm[j][None], kv_norm=mla_kv_norm[j][None],
                     w_uq=w_uq.astype(BF16),
                     w_uk=jnp.transpose(mla_w_uk[j], (1, 2, 0)).astype(BF16),
                     w_uv=jnp.transpose(mla_w_uv[j], (1, 0, 2)).astype(BF16),
                     w_o=mla_w_o[j].astype(BF16))
            for x, sh, sc, gt, pos, tm, tag in ((xp, shp, scp, gtp, pos_p, tm_p, "p"), (xs, shs, scs, gts, pos_s, tm_s, "s")):
                tq = _rope_tables(pos, MLA_ROPE_THETA, MLA_ROPE, MLA_ROPE, hd)
                tk = _rope_tables(pos, MLA_ROPE_THETA, MLA_ROPE, LANES, 1)
                if tag == "p":
                    ckv, kpe, kcat, vt, qcat = _mla_proj(x, sh, sc, w, tq, tk, b, s, tm)
                    o = _mla_attn(qcat, kcat, vt, b, s)
                    outs["ckv_p"].append(ckv.reshape(b, s, MLA_KV_RANK))
                    outs["kpe_p"].append(kpe.reshape(b, s, MLA_ROPE))
                else:
                    ckv, kpe, _, _, qcat = _mla_proj(x, sh, sc, w, tq, tk, 1, db, tm)
                    qcat = qcat.reshape(db, hd, MLA_QK_WIDTH)
                    rpad = ((0, 0), (0, MLA_DEC_ROWS - hd), (0, 0))
                    o = _mla_dec(page_table,
                                 jnp.pad(qcat[:, :, :MLA_KV_RANK], rpad),
                                 jnp.pad(qcat[:, :, MLA_KV_RANK:MLA_KV_RANK + MLA_ROPE], rpad),
                                 ckv[:, None, :], kpe[:, None, :], cache_mla_ckv, cache_kpe_t, j)
                    o = o[:, :hd].reshape(db, hd * MLA_KV_RANK)
                    outs["ckv_s"].append(ckv.reshape(db, 1, MLA_KV_RANK))
                    outs["kpe_s"].append(kpe.reshape(db, 1, MLA_ROPE))
                y = _mla_out(o, w, x, gt, g0, b0, alpha, tm)
                if tag == "p":
                    xp = y
                else:
                    xs = y
        elif kind == 1:
            w_in = ret_w_in[j].astype(BF16)
            w_o = ret_w_o[j].astype(BF16)
            gn = ret_gn[j][None]
            tp = _rope_tables(pos_p, RET_ROPE_THETA, RET_DK, RET_DK, RET_HEADS)
            ts = _rope_tables(pos_s, RET_ROPE_THETA, RET_DK, RET_DK, RET_HEADS)
            zp = _ret_proj(xp, shp, scp, w_in, tp, tm_p)
            op, sfin = _ret_chunks(zp, b, s, ret_tabs)
            xp = _ret_out(op, zp, gn, w_o, xp, gtp, g0, b0, alpha, tm_p)
            zs = _ret_proj(xs, shs, scs, w_in, ts, tm_s)
            os_, snew = _ret_step(state_ret[j], zs, ret_tabs[4])
            xs = _ret_out(os_, zs, gn, w_o, xs, gts, g0, b0, alpha, tm_s)
            outs["rs_p"].append(sfin)
            outs["rs_s"].append(snew)
        else:
            hd, dh = MOBA_HEADS, MOBA_HEAD_DIM
            w_qkv = moba_w_qkv[j].astype(BF16)
            w_o = moba_w_o[j].astype(BF16)
            tp = _rope_tables(pos_p, MOBA_ROPE_THETA, MOBA_ROT_DIM, dh, hd)
            ts = _rope_tables(pos_s, MOBA_ROPE_THETA, MOBA_ROT_DIM, dh, hd)
            q, k, v, kb, vtr, kmean = _moba_proj(xp, shp, scp, w_qkv, tp, b, s, tm_p)
            nb = s // MOBA_BLOCK
            kmean = jnp.pad(jnp.transpose(kmean, (0, 2, 1, 3)), ((0, 0), (0, 0), (0, LANES - nb), (0, 0)))
            op = _moba_attn(q, kb, vtr, kmean, b, s)
            xp = _proj_out(op, w_o, xp, gtp, g0, b0, alpha, tm_p)
            outs["mk_p"].append(k)
            outs["mv_p"].append(v)
            qs, ks, vs, _, _, _ = _moba_proj(xs, shs, scs, w_qkv, ts, 1, db, tm_s)
            qs = qs.astype(F32).reshape(db, hd, dh)
            ks = jnp.transpose(ks[0], (1, 0, 2))[:, :, None, :]
            vs = jnp.transpose(vs[0], (1, 0, 2))[:, :, None, :]
            sel = _moba_dec_select(page_table, qs, cache_moba_k, j)
            os_ = _moba_dec_attn(page_table, sel, qs[:, :, None, :], ks, vs, cache_moba_k, cache_moba_v, j)
            xs = _proj_out(os_.reshape(db, hd * dh), w_o, xs, gts, g0, b0, alpha, tm_s)
            outs["mk_s"].append(ks)
            outs["mv_s"].append(vs)

        (shp, scp, gtp), (shs, scs, gts) = _mods(mod, 2 * i + 1, b, db, d)
        g1, b1 = ln_g[i, 1][None], ln_b[i, 1][None]
        w_q = peer_w_q[i].astype(BF16)
        k1 = peer_k1[i].astype(BF16)
        k2 = peer_k2[i].astype(BF16)
        u = peer_u[i].astype(BF16)
        te = PEER_DENSE_KEYS * PEER_N_KEYS
        vt = jnp.transpose(peer_v[i].reshape(-1, te, d), (0, 2, 1)).astype(BF16)
        hp, *selp = _peer_topk(xp, shp, scp, w_q, k1, k2, tm_p)
        xp = _peer_dense(hp, u, vt, selp, xp, gtp, g1, b1, alpha, min(PEER_DENSE_TOKENS, s), PEER_DENSE_KEYS)
        hs, *sels = _peer_topk(xs, shs, scs, w_q, k1, k2, tm_s)
        xs = _peer_dense(hs, u, vt, sels, xs, gts, g1, b1, alpha, tm_s, PEER_DENSE_KEYS)

    return (xp.reshape(b, s, d), xs.reshape(db, 1, d),
            jnp.stack(outs["ckv_p"]), jnp.stack(outs["kpe_p"]), jnp.stack(outs["ckv_s"]), jnp.stack(outs["kpe_s"]),
            jnp.stack(outs["mk_p"]), jnp.stack(outs["mv_p"]), jnp.stack(outs["mk_s"]), jnp.stack(outs["mv_s"]),
            jnp.stack(outs["rs_p"]), jnp.stack(outs["rs_s"]))
```
